```python
import math
import jax, jax.numpy as jnp
from jax import lax
import numpy as np

D_MODEL = 1024
BATCH = 2
SEQ = 8192
DEPTH = 1
DEC_BATCH = 32
DEC_SEQ = 4
PAST_LEN = 8192
PAGE_SIZE = 128

HEAD_DIM = 64
N_HEADS = D_MODEL // HEAD_DIM
N_DIFF_HEADS = N_HEADS // 2
N_FOX_HEADS = N_HEADS - N_DIFF_HEADS
DIFF_D = HEAD_DIM // 2
DIFF_W = N_DIFF_HEADS * HEAD_DIM
FOX_W = N_FOX_HEADS * HEAD_DIM
ROT_DIM = DIFF_D // 4
ROPE_THETA = 500000.0
Q_BLOCK = 128
FORGET_BIAS_INIT = 3.0
N_EXPERTS = 64
N_EXPERT_GROUPS = 8
TOPK_GROUPS = 4
TOP_K = 8
D_EXPERT = 256
D_SHARED = 256
ROUTED_SCALE = 2.5
MOE_TOKEN_BLOCK = 128
DEEPNORM_ALPHA = (2.0 * DEPTH) ** 0.25
DEEPNORM_BETA = (8.0 * DEPTH) ** -0.25
LN_EPS = 1e-5
RMS_EPS = 1e-5
NEG_INF = -1e30
IN_SIZES = (DIFF_W, DIFF_W, DIFF_W, FOX_W, FOX_W, FOX_W, N_FOX_HEADS)
D_IN = sum(IN_SIZES)

kernel_name = 'hymba_diffattn_fox_moe_deepnorm_step'


def lambda_init(layer):
    return 0.8 - 0.6 * math.exp(-0.3 * layer)


def layer_norm(x, g, b):
    xf = x.astype(jnp.float32)
    mu = jnp.mean(xf, axis=-1, keepdims=True)
    var = jnp.mean(jnp.square(xf - mu), axis=-1, keepdims=True)
    return ((xf - mu) * lax.rsqrt(var + LN_EPS) * g + b).astype(x.dtype)


def partial_rope(x, pos):
    half = ROT_DIM // 2
    inv_freq = jnp.power(ROPE_THETA, -jnp.arange(half, dtype=jnp.float32) * 2.0 / ROT_DIM)
    ang = pos.astype(jnp.float32)[:, None] * inv_freq[None, :]
    cos = jnp.cos(ang)[None, :, None, None, :].astype(x.dtype)
    sin = jnp.sin(ang)[None, :, None, None, :].astype(x.dtype)
    x1 = x[..., :half]
    x2 = x[..., half:ROT_DIM]
    return jnp.concatenate([x1 * cos - x2 * sin, x2 * cos + x1 * sin, x[..., ROT_DIM:]], axis=-1)


def project(x, w_in, b_forget, pos):
    B, T = x.shape[0], x.shape[1]
    z = jnp.einsum('btd,de->bte', x, w_in)
    splits = np.cumsum(IN_SIZES)[:-1].tolist()
    qd, kd, vd, qf, kf, vf, fl = jnp.split(z, splits, axis=-1)
    qd = partial_rope(qd.reshape(B, T, N_DIFF_HEADS, 2, DIFF_D), pos)
    kd = partial_rope(kd.reshape(B, T, N_DIFF_HEADS, 2, DIFF_D), pos)
    vd = vd.reshape(B, T, N_DIFF_HEADS, HEAD_DIM)
    qf = qf.reshape(B, T, N_FOX_HEADS, HEAD_DIM)
    kf = kf.reshape(B, T, N_FOX_HEADS, HEAD_DIM)
    vf = vf.reshape(B, T, N_FOX_HEADS, HEAD_DIM)
    logf = jax.nn.log_sigmoid((fl + b_forget).astype(jnp.float32))
    return qd, kd, vd, qf, kf, vf, logf


def diff_attention(q, k, v, lam, q_pos, k_pos):
    s = jnp.einsum('bthcd,blhcd->bchtl', q, k).astype(jnp.float32) * (DIFF_D ** -0.5)
    causal = k_pos[None, :] <= q_pos[:, None]
    p = jax.nn.softmax(jnp.where(causal, s, NEG_INF), axis=-1)
    w = p[:, 0] - lam * p[:, 1]
    return jnp.einsum('bhtl,blhe->bthe', w.astype(v.dtype), v)


def forgetting_attention(q, k, v, cq, ck, q_pos, k_pos):
    s = jnp.einsum('bthe,blhe->bhtl', q, k).astype(jnp.float32) * (HEAD_DIM ** -0.5)
    s = s + jnp.transpose(cq, (0, 2, 1))[..., :, None] - jnp.transpose(ck, (0, 2, 1))[..., None, :]
    causal = k_pos[None, :] <= q_pos[:, None]
    p = jax.nn.softmax(jnp.where(causal, s, NEG_INF), axis=-1)
    return jnp.einsum('bhtl,blhe->bthe', p.astype(v.dtype), v)


def merge_heads(od, of, subln_gain, w_o, lam_init):
    B, T = od.shape[0], od.shape[1]
    odf = od.astype(jnp.float32)
    odn = odf * lax.rsqrt(jnp.mean(jnp.square(odf), axis=-1, keepdims=True) + RMS_EPS) * subln_gain * (1.0 - lam_init)
    o = jnp.concatenate([odn.astype(od.dtype).reshape(B, T, DIFF_W), of.reshape(B, T, FOX_W)], axis=-1)
    return jnp.einsum('btd,de->bte', o, w_o)


def moe_ffn(h, w_router, router_bias, w_gate, w_up, w_down, w_sh_gate, w_sh_up, w_sh_down):
    N = h.shape[0]
    scores = jax.nn.sigmoid(jnp.einsum('nd,de->ne', h, w_router).astype(jnp.float32))
    choice = scores + router_bias.astype(jnp.float32)
    grp = choice.reshape(N, N_EXPERT_GROUPS, N_EXPERTS // N_EXPERT_GROUPS)
    grp_score = jnp.sum(lax.top_k(grp, 2)[0], axis=-1)
    _, top_g = lax.top_k(grp_score, TOPK_GROUPS)
    gmask = jnp.sum(jax.nn.one_hot(top_g, N_EXPERT_GROUPS, dtype=jnp.float32), axis=1) > 0
    emask = jnp.repeat(gmask, N_EXPERTS // N_EXPERT_GROUPS, axis=1)
    _, top_e = lax.top_k(jnp.where(emask, choice, NEG_INF), TOP_K)
    w_sel = jnp.take_along_axis(scores, top_e, axis=1)
    w_sel = w_sel / (jnp.sum(w_sel, axis=-1, keepdims=True) + 1e-20) * ROUTED_SCALE
    gates = jnp.sum(jax.nn.one_hot(top_e, N_EXPERTS, dtype=jnp.float32) * w_sel[..., None], axis=1)
    a = jnp.einsum('nd,edf->nef', h, w_gate)
    u = jnp.einsum('nd,edf->nef', h, w_up)
    act = jax.nn.silu(a) * u * gates[..., None].astype(h.dtype)
    y = jnp.einsum('nef,efd->nd', act, w_down)
    sh = jnp.einsum('nf,fd->nd', jax.nn.silu(h @ w_sh_gate) * (h @ w_sh_up), w_sh_down)
    return y + sh


def gather_pages(pool, page_table):
    g = pool[page_table]
    return g.reshape((g.shape[0], g.shape[1] * g.shape[2]) + g.shape[3:])


def setup_inputs(seed: int = 0) -> dict:
    key = jax.random.key(seed)
    ks = jax.random.split(key, 32)
    f32 = jnp.float32
    n_pages = PAST_LEN // PAGE_SIZE
    n_used = DEC_BATCH * n_pages
    n_pool = n_used + max(1, n_used // 4)
    page_table = jax.random.permutation(ks[0], n_pool)[:n_used].reshape(DEC_BATCH, n_pages).astype(jnp.int32)

    def nrm(k, shape, scale=1.0):
        return scale * jax.random.normal(k, shape, f32)

    pool = (DEPTH, n_pool, PAGE_SIZE)
    return {
        'x_prompt': nrm(ks[1], (BATCH, SEQ, D_MODEL)),
        'x_sample': nrm(ks[2], (DEC_BATCH, DEC_SEQ, D_MODEL)),
        'cache_diff_k': nrm(ks[3], pool + (N_DIFF_HEADS, 2, DIFF_D)),
        'cache_diff_v': nrm(ks[4], pool + (N_DIFF_HEADS, HEAD_DIM)),
        'cache_fox_k': nrm(ks[5], pool + (N_FOX_HEADS, HEAD_DIM)),
        'cache_fox_v': nrm(ks[6], pool + (N_FOX_HEADS, HEAD_DIM)),
        'cache_fox_logf': jax.nn.log_sigmoid(FORGET_BIAS_INIT + nrm(ks[7], pool + (N_FOX_HEADS,))),
        'page_table': page_table,
        'w_in': nrm(ks[8], (DEPTH, D_MODEL, D_IN), D_MODEL ** -0.5),
        'b_forget': FORGET_BIAS_INIT + nrm(ks[9], (DEPTH, N_FOX_HEADS), 0.1),
        'lambda_q1': nrm(ks[10], (DEPTH, DIFF_D), 0.1),
        'lambda_k1': nrm(ks[11], (DEPTH, DIFF_D), 0.1),
        'lambda_q2': nrm(ks[12], (DEPTH, DIFF_D), 0.1),
        'lambda_k2': nrm(ks[13], (DEPTH, DIFF_D), 0.1),
        'subln_gain': 1.0 + nrm(ks[14], (DEPTH, HEAD_DIM), 0.02),
        'w_o': nrm(ks[15], (DEPTH, D_MODEL, D_MODEL), D_MODEL ** -0.5 * DEEPNORM_BETA),
        'ln1_g': 1.0 + nrm(ks[16], (DEPTH, D_MODEL), 0.02),
        'ln1_b': nrm(ks[17], (DEPTH, D_MODEL), 0.02),
        'w_router': nrm(ks[18], (DEPTH, D_MODEL, N_EXPERTS), D_MODEL ** -0.5),
        'router_bias': nrm(ks[19], (DEPTH, N_EXPERTS), 0.01),
        'w_exp_gate': nrm(ks[20], (DEPTH, N_EXPERTS, D_MODEL, D_EXPERT), D_MODEL ** -0.5),
        'w_exp_up': nrm(ks[21], (DEPTH, N_EXPERTS, D_MODEL, D_EXPERT), D_MODEL ** -0.5),
        'w_exp_down': nrm(ks[22], (DEPTH, N_EXPERTS, D_EXPERT, D_MODEL), D_EXPERT ** -0.5 * DEEPNORM_BETA),
        'w_sh_gate': nrm(ks[23], (DEPTH, D_MODEL, D_SHARED), D_MODEL ** -0.5),
        'w_sh_up': nrm(ks[24], (DEPTH, D_MODEL, D_SHARED), D_MODEL ** -0.5),
        'w_sh_down': nrm(ks[25], (DEPTH, D_SHARED, D_MODEL), D_SHARED ** -0.5 * DEEPNORM_BETA),
        'ln2_g': 1.0 + nrm(ks[26], (DEPTH, D_MODEL), 0.02),
        'ln2_b': nrm(ks[27], (DEPTH, D_MODEL), 0.02),
    }


def reference(x_prompt, x_sample, cache_diff_k, cache_diff_v, cache_fox_k, cache_fox_v, cache_fox_logf,
              page_table, w_in, b_forget, lambda_q1, lambda_k1, lambda_q2, lambda_k2, subln_gain, w_o,
              ln1_g, ln1_b, w_router, router_bias, w_exp_gate, w_exp_up, w_exp_down,
              w_sh_gate, w_sh_up, w_sh_down, ln2_g, ln2_b):
    f32 = jnp.float32
    n_blocks = SEQ // Q_BLOCK
    pos_p = jnp.arange(SEQ, dtype=jnp.int32)
    pos_s = PAST_LEN + jnp.arange(DEC_SEQ, dtype=jnp.int32)
    kpos_s = jnp.arange(PAST_LEN + DEC_SEQ, dtype=jnp.int32)
    xp, xs = x_prompt, x_sample
    p_dk, p_dv, p_fk, p_fv, p_fl = [], [], [], [], []
    s_dk, s_dv, s_fk, s_fv, s_fl = [], [], [], [], []
    for l in range(DEPTH):
        lam_init = lambda_init(l)
        lam = (jnp.exp(jnp.sum((lambda_q1[l] * lambda_k1[l]).astype(f32)))
               - jnp.exp(jnp.sum((lambda_q2[l] * lambda_k2[l]).astype(f32))) + lam_init)
        moe_w = (w_router[l], router_bias[l], w_exp_gate[l], w_exp_up[l], w_exp_down[l],
                 w_sh_gate[l], w_sh_up[l], w_sh_down[l])

        qd, kd, vd, qf, kf, vf, logf = project(xp, w_in[l], b_forget[l], pos_p)
        c = jnp.cumsum(logf, axis=1)

        def prompt_block(i):
            start = i * Q_BLOCK
            q_pos = start + jnp.arange(Q_BLOCK, dtype=jnp.int32)
            od_b = diff_attention(lax.dynamic_slice_in_dim(qd, start, Q_BLOCK, axis=1), kd, vd, lam, q_pos, pos_p)
            of_b = forgetting_attention(lax.dynamic_slice_in_dim(qf, start, Q_BLOCK, axis=1), kf, vf,
                                        lax.dynamic_slice_in_dim(c, start, Q_BLOCK, axis=1), c, q_pos, pos_p)
            return od_b, of_b

        od, of = lax.map(prompt_block, jnp.arange(n_blocks, dtype=jnp.int32))
        bp = xp.shape[0]
        od = jnp.moveaxis(od, 0, 1).reshape(bp, SEQ, N_DIFF_HEADS, HEAD_DIM)
        of = jnp.moveaxis(of, 0, 1).reshape(bp, SEQ, N_FOX_HEADS, HEAD_DIM)
        hp = layer_norm(DEEPNORM_ALPHA * xp + merge_heads(od, of, subln_gain[l], w_o[l], lam_init), ln1_g[l], ln1_b[l])
        moe_p = lax.map(lambda blk: moe_ffn(blk, *moe_w), hp.reshape(-1, MOE_TOKEN_BLOCK, D_MODEL)).reshape(hp.shape)
        xp = layer_norm(DEEPNORM_ALPHA * hp + moe_p, ln2_g[l], ln2_b[l])
        p_dk.append(kd); p_dv.append(vd); p_fk.append(kf); p_fv.append(vf); p_fl.append(logf)

        sqd, skd, svd, sqf, skf, svf, slogf = project(xs, w_in[l], b_forget[l], pos_s)
        kd_all = jnp.concatenate([gather_pages(cache_diff_k[l], page_table), skd], axis=1)
        vd_all = jnp.concatenate([gather_pages(cache_diff_v[l], page_table), svd], axis=1)
        kf_all = jnp.concatenate([gather_pages(cache_fox_k[l], page_table), skf], axis=1)
        vf_all = jnp.concatenate([gather_pages(cache_fox_v[l], page_table), svf], axis=1)
        c_all = jnp.cumsum(jnp.concatenate([gather_pages(cache_fox_logf[l], page_table).astype(f32), slogf], axis=1), axis=1)
        od_s = diff_attention(sqd, kd_all, vd_all, lam, pos_s, kpos_s)
        of_s = forgetting_attention(sqf, kf_all, vf_all, c_all[:, PAST_LEN:], c_all, pos_s, kpos_s)
        hs = layer_norm(DEEPNORM_ALPHA * xs + merge_heads(od_s, of_s, subln_gain[l], w_o[l], lam_init), ln1_g[l], ln1_b[l])
        moe_s = moe_ffn(hs.reshape(-1, D_MODEL), *moe_w).reshape(hs.shape)
        xs = layer_norm(DEEPNORM_ALPHA * hs + moe_s, ln2_g[l], ln2_b[l])
        s_dk.append(skd); s_dv.append(svd); s_fk.append(skf); s_fv.append(svf); s_fl.append(slogf)

    return (xp, xs,
            jnp.stack(p_dk), jnp.stack(p_dv), jnp.stack(p_fk), jnp.stack(p_fv), jnp.stack(p_fl),
            jnp.stack(s_dk), jnp.stack(s_dv), jnp.stack(s_fk), jnp.stack(s_fv), jnp.stack(s_fl))
```

```python
import functools
import math

import jax
import jax.numpy as jnp
from jax import lax
from jax.experimental import pallas as pl
from jax.experimental.pallas import tpu as pltpu

F32 = jnp.float32
BF16 = jnp.bfloat16

HEAD_DIM = 64
N_DIFF_HEADS = 8
N_FOX_HEADS = 8
DIFF_D = HEAD_DIM // 2
DIFF_W = N_DIFF_HEADS * HEAD_DIM
FOX_W = N_FOX_HEADS * HEAD_DIM
ROT_DIM = DIFF_D // 4
ROT_HALF = ROT_DIM // 2
ROPE_THETA = 500000.0
N_EXPERTS = 64
N_EXPERT_GROUPS = 8
GROUP_SIZE = N_EXPERTS // N_EXPERT_GROUPS
TOPK_GROUPS = 4
TOP_K = 8
ROUTED_SCALE = 2.5
LN_EPS = 1e-5
RMS_EPS = 1e-5
NEG_INF = -1e30

LANES = 128
SUBLANES = 8
BF16_ROWS = 16
V_ROWS = HEAD_DIM + BF16_ROWS
VMEM_LIMIT = 56 * 1024 * 1024

NT_DIMS = (((1,), (1,)), ((), ()))


def _cparams(sem):
    return pltpu.CompilerParams(dimension_semantics=sem, vmem_limit_bytes=VMEM_LIMIT)


def _log_sigmoid(x):
    return jnp.minimum(x, 0.0) - jnp.log1p(jnp.exp(-jnp.abs(x)))


def _layer_norm(y, g, b):
    mu = jnp.mean(y, axis=-1, keepdims=True)
    d = y - mu
    var = jnp.mean(d * d, axis=-1, keepdims=True)
    return d * lax.rsqrt(var + LN_EPS) * g + b


def _split3(a):
    a1 = a.astype(BF16)
    r1 = a - a1.astype(F32)
    a2 = r1.astype(BF16)
    a3 = (r1 - a2.astype(F32)).astype(BF16)
    return a1, a2, a3


def _dot_sel_lhs(sel, b):
    return sum(jnp.dot(sel, t, preferred_element_type=F32) for t in _split3(b))


def _dot_sel_rhs(a, sel):
    return sum(jnp.dot(t, sel, preferred_element_type=F32) for t in _split3(a))


def _lambda_full(lq1_ref, lk1_ref, lq2_ref, lk2_ref, lam_init):
    a = jnp.sum(lq1_ref[...] * lk1_ref[...], axis=1, keepdims=True)
    b = jnp.sum(lq2_ref[...] * lk2_ref[...], axis=1, keepdims=True)
    return jnp.exp(a) - jnp.exp(b) + lam_init


def _rope_table_kernel(invf_lane_ref, invf_sub_ref, c_ref, s1_ref, s2_ref, ct_ref, st_ref, *, base, mod, tm):
    i = pl.program_id(0)
    row = lax.broadcasted_iota(jnp.int32, (tm, LANES), 0) + i * tm
    pos = (base + row % mod).astype(F32)
    ang = pos * invf_lane_ref[...]
    li = lax.broadcasted_iota(jnp.int32, (tm, LANES), 1) % DIFF_D
    c = jnp.cos(ang)
    s = jnp.sin(ang)
    c_ref[...] = jnp.where(li < ROT_DIM, c, 1.0)
    s1_ref[...] = jnp.where(li < ROT_HALF, -s, 0.0)
    s2_ref[...] = jnp.where((li >= ROT_HALF) & (li < ROT_DIM), s, 0.0)
    col = lax.broadcasted_iota(jnp.int32, (SUBLANES, tm), 1) + i * tm
    pos_t = (base + col % mod).astype(F32)
    ang_t = invf_sub_ref[...] * pos_t
    sub = lax.broadcasted_iota(jnp.int32, (SUBLANES, tm), 0)
    ct_ref[...] = jnp.cos(ang_t)
    st_ref[...] = jnp.where(sub < ROT_HALF, -jnp.sin(ang_t), jnp.sin(ang_t))


def _rope_tables(n_pos, base, mod):
    inv4 = jnp.power(ROPE_THETA, -jnp.arange(ROT_HALF, dtype=F32) * 2.0 / ROT_DIM)
    invf_lane = inv4[(jnp.arange(LANES) % DIFF_D) % ROT_HALF][None, :]
    invf_sub = inv4[jnp.arange(SUBLANES) % ROT_HALF][:, None]
    tm = min(n_pos, 1024)
    assert n_pos % tm == 0
    lane_spec = pl.BlockSpec((tm, LANES), lambda i: (i, 0))
    sub_spec = pl.BlockSpec((SUBLANES, tm), lambda i: (0, i))
    return pl.pallas_call(
        functools.partial(_rope_table_kernel, base=base, mod=mod, tm=tm),
        grid=(n_pos // tm,),
        in_specs=[pl.BlockSpec((1, LANES), lambda i: (0, 0)), pl.BlockSpec((SUBLANES, 1), lambda i: (0, 0))],
        out_specs=[lane_spec, lane_spec, lane_spec, sub_spec, sub_spec],
        out_shape=[jax.ShapeDtypeStruct((n_pos, LANES), F32)] * 3 + [jax.ShapeDtypeStruct((SUBLANES, n_pos), F32)] * 2,
        compiler_params=_cparams(("parallel",)),
        name="rope_tables",
    )(invf_lane, invf_sub)


def _proj_kernel(x_ref, wkv_ref, wfl_ref, wtq_ref, wtv_ref, wtfl_ref, bfl_ref, btfl_ref,
                 c_ref, s1_ref, s2_ref, ct_ref, st_ref,
                 kd_ref, vd_ref, kf_ref, vf_ref, logf_ref, kdb_ref, kfb_ref,
                 qdt_ref, qft_ref, vdt_ref, vft_ref, logft_ref, *, tm, tk):
    xb = x_ref[...].astype(BF16)
    z = jnp.dot(xb, wkv_ref[...], preferred_element_type=F32)
    c, s1, s2 = c_ref[...], s1_ref[...], s2_ref[...]
    pieces = []
    for j in range(DIFF_W // LANES):
        xk = z[:, j * LANES:(j + 1) * LANES]
        pieces.append(xk * c + pltpu.roll(xk, LANES - ROT_HALF, 1) * s1 + pltpu.roll(xk, ROT_HALF, 1) * s2)
    kd = jnp.concatenate(pieces, axis=1)
    kd_ref[...] = kd
    kdb_ref[...] = kd.astype(BF16)
    vd_ref[...] = z[:, DIFF_W:2 * DIFF_W]
    kf = z[:, 2 * DIFF_W:2 * DIFF_W + FOX_W]
    kf_ref[...] = kf
    kfb_ref[...] = kf.astype(BF16)
    vf_ref[...] = z[:, 2 * DIFF_W + FOX_W:]
    fl = jnp.dot(xb, wfl_ref[...], preferred_element_type=F32) + bfl_ref[...]
    logf_ref[...] = _log_sigmoid(fl)[:, :N_FOX_HEADS]

    zq = lax.dot_general(wtq_ref[...], xb, NT_DIMS, preferred_element_type=F32)
    ct, st = ct_ref[...], st_ref[...]
    rows = []
    for j in range(DIFF_W // DIFF_D):
        x8 = zq[DIFF_D * j:DIFF_D * j + ROT_DIM]
        rows.append(x8 * ct + pltpu.roll(x8, ROT_HALF, 0) * st)
        rows.append(zq[DIFF_D * j + ROT_DIM:DIFF_D * (j + 1)])
    qd = jnp.concatenate(rows, axis=0) * (DIFF_D ** -0.5)
    qdt_ref[0] = qd.astype(BF16)
    qft_ref[0] = (zq[DIFF_W:] * (HEAD_DIM ** -0.5)).astype(BF16)

    zv = lax.dot_general(wtv_ref[...], xb, NT_DIMS, preferred_element_type=F32)
    ones = jnp.ones((BF16_ROWS, tk), BF16)
    for h in range(N_DIFF_HEADS):
        for t in range(tm // tk):
            vdt_ref[0, h, t, 0:HEAD_DIM, :] = zv[HEAD_DIM * h:HEAD_DIM * (h + 1), t * tk:(t + 1) * tk].astype(BF16)
            vdt_ref[0, h, t, HEAD_DIM:V_ROWS, :] = ones
            vft_ref[0, h, t, 0:HEAD_DIM, :] = zv[DIFF_W + HEAD_DIM * h:DIFF_W + HEAD_DIM * (h + 1),
                                                 t * tk:(t + 1) * tk].astype(BF16)
            vft_ref[0, h, t, HEAD_DIM:V_ROWS, :] = ones
    flt = lax.dot_general(wtfl_ref[...], xb, NT_DIMS, preferred_element_type=F32)
    logft_ref[0] = _log_sigmoid(flt[:N_FOX_HEADS] + btfl_ref[...])


def _project(x2d, n_batch, seq, w, tables, tm, tk):
    n, d_model = x2d.shape
    nt = seq // tm
    c, s1, s2, ct, st = tables
    full = lambda a: pl.BlockSpec(a.shape, lambda r: (0,) * a.ndim)
    row512 = pl.BlockSpec((tm, DIFF_W), lambda r: (r, 0))
    tab = pl.BlockSpec((tm, LANES), lambda r: (r % nt, 0))
    tabt = pl.BlockSpec((SUBLANES, tm), lambda r: (0, r % nt))
    qt_spec = pl.BlockSpec((1, DIFF_W, tm), lambda r: (r // nt, 0, r % nt))
    vt_spec = pl.BlockSpec((1, N_DIFF_HEADS, tm // tk, V_ROWS, tk), lambda r: (r // nt, 0, r % nt, 0, 0))
    lt_spec = pl.BlockSpec((1, N_FOX_HEADS, tm), lambda r: (r // nt, 0, r % nt))
    sds = jax.ShapeDtypeStruct
    return pl.pallas_call(
        functools.partial(_proj_kernel, tm=tm, tk=tk),
        grid=(n // tm,),
        in_specs=[pl.BlockSpec((tm, d_model), lambda r: (r, 0)),
                  full(w["wkv"]), full(w["wfl"]), full(w["wtq"]), full(w["wtv"]), full(w["wtfl"]),
                  full(w["bfl"]), full(w["btfl"]), tab, tab, tab, tabt, tabt],
        out_specs=[row512, row512, row512, row512, pl.BlockSpec((tm, N_FOX_HEADS), lambda r: (r, 0)),
                   row512, row512, qt_spec, qt_spec, vt_spec, vt_spec, lt_spec],
        out_shape=[sds((n, DIFF_W), F32)] * 4 + [sds((n, N_FOX_HEADS), F32)] + [sds((n, DIFF_W), BF16)] * 2
                  + [sds((n_batch, DIFF_W, seq), BF16)] * 2
                  + [sds((n_batch, N_DIFF_HEADS, seq // tk, V_ROWS, tk), BF16)] * 2
                  + [sds((n_batch, N_FOX_HEADS, seq), F32)],
        compiler_params=_cparams(("parallel",)),
        name="in_proj",
    )(x2d, w["wkv"], w["wfl"], w["wtq"], w["wtv"], w["wtfl"], w["bfl"], w["btfl"], c, s1, s2, ct, st)


def _lane_cumsum(x):
    lane = lax.broadcasted_iota(jnp.int32, x.shape, 1)
    s = 1
    while s < LANES:
        x = x + jnp.where(lane >= s, pltpu.roll(x, s, 1), 0.0)
        s *= 2
    return x


def _lane_rev_cumsum(x):
    lane = lax.broadcasted_iota(jnp.int32, x.shape, 1)
    s = 1
    while s < LANES:
        x = x + jnp.where(lane < LANES - s, pltpu.roll(x, LANES - s, 1), 0.0)
        s *= 2
    return x


def _cumsum_kernel(lft_ref, ct_ref, *, seq):
    carry = jnp.zeros((N_FOX_HEADS, 1), F32)
    for t in range(seq // LANES):
        x = _lane_cumsum(lft_ref[0, :, t * LANES:(t + 1) * LANES]) + carry
        ct_ref[0, :, t * LANES:(t + 1) * LANES] = x
        carry = x[:, LANES - 1:LANES]


def _cumsum_t(logft):
    nb, nh, seq = logft.shape
    spec = pl.BlockSpec((1, nh, seq), lambda b: (b, 0, 0))
    return pl.pallas_call(
        functools.partial(_cumsum_kernel, seq=seq),
        grid=(nb,), in_specs=[spec], out_specs=spec,
        out_shape=jax.ShapeDtypeStruct(logft.shape, F32),
        compiler_params=_cparams(("parallel",)),
        name="logf_cumsum",
    )(logft)


def _block_diag_q(qt, nblk, tq):
    rows_per = LANES // nblk
    rb = lax.broadcasted_iota(jnp.int32, qt.shape, 0) // rows_per
    zero = jnp.zeros_like(qt)
    return jnp.concatenate([jnp.where(rb == j, qt, zero) for j in range(nblk)], axis=1)


def _attn_tile(j, k_ref, vt_ref, qbd, acc_ref, m_ref, *, tq, nblk, blk_head, masked, ck_ref=None, cq=None):
    tk = tq
    start = pl.multiple_of(j * tk, tk)
    k = k_ref[0, pl.ds(start, tk), :]
    s = jnp.dot(k, qbd, preferred_element_type=F32)
    if masked:
        keep = (lax.broadcasted_iota(jnp.int32, (tk, tq), 0) <= lax.broadcasted_iota(jnp.int32, (tk, tq), 1))
    if ck_ref is not None:
        ck = ck_ref[0, 0, pl.ds(start, tk), :]
    for blk in range(nblk):
        sb = s[:, blk * tq:(blk + 1) * tq]
        if ck_ref is not None:
            sb = sb - ck[:, blk:blk + 1]
        if masked:
            sb = jnp.where(keep, sb, NEG_INF)
        m_old = m_ref[blk, 0:1, :]
        smax = jnp.max(sb, axis=0, keepdims=True)
        if cq is not None:
            m_new = jnp.maximum(m_old, smax + cq[blk])
            shift = m_new - cq[blk]
        else:
            m_new = jnp.maximum(m_old, smax)
            shift = m_new
        p = jnp.exp(sb - shift).astype(BF16)
        alpha = jnp.exp(m_old - m_new)
        vt = vt_ref[0, blk_head[blk], j]
        acc_ref[blk] = acc_ref[blk] * alpha + jnp.dot(vt, p, preferred_element_type=F32)
        m_ref[blk] = jnp.broadcast_to(m_new, (SUBLANES, tq))


def _attn_init(acc_ref, m_ref):
    acc_ref[...] = jnp.zeros(acc_ref.shape, F32)
    m_ref[...] = jnp.full(m_ref.shape, NEG_INF, F32)


def _diff_attn_kernel(lq1_ref, lk1_ref, lq2_ref, lk2_ref, gain_ref, qt_ref, k_ref, vt_ref, o_ref, acc_ref, m_ref,
                      *, tq, lam_init):
    i = pl.program_id(2)
    nblk = 4
    qbd = _block_diag_q(qt_ref[0], nblk, tq)
    _attn_init(acc_ref, m_ref)
    tile = functools.partial(_attn_tile, k_ref=k_ref, vt_ref=vt_ref, qbd=qbd, acc_ref=acc_ref, m_ref=m_ref,
                             tq=tq, nblk=nblk, blk_head=(0, 0, 1, 1))

    def body(j, carry):
        tile(j, masked=False)
        return carry

    lax.fori_loop(0, i, body, 0)
    tile(i, masked=True)

    lam = _lambda_full(lq1_ref, lk1_ref, lq2_ref, lk2_ref, lam_init)
    outs = []
    for hh in range(2):
        a1 = acc_ref[2 * hh]
        a2 = acc_ref[2 * hh + 1]
        o = a1[0:HEAD_DIM] / a1[HEAD_DIM:HEAD_DIM + 1] - lam * (a2[0:HEAD_DIM] / a2[HEAD_DIM:HEAD_DIM + 1])
        ms = jnp.mean(o * o, axis=0, keepdims=True)
        outs.append(o * lax.rsqrt(ms + RMS_EPS))
    o = jnp.concatenate(outs, axis=0).T
    o_ref[0] = (o * gain_ref[...] * (1.0 - lam_init)).astype(o_ref.dtype)


def _fox_attn_kernel(qt_ref, k_ref, vt_ref, ck_ref, cqt_ref, o_ref, acc_ref, m_ref, *, tq):
    i = pl.program_id(2)
    nblk = 2
    qbd = _block_diag_q(qt_ref[0], nblk, tq)
    _attn_init(acc_ref, m_ref)
    cq = [cqt_ref[0, 0, hh:hh + 1, :] for hh in range(nblk)]
    tile = functools.partial(_attn_tile, k_ref=k_ref, vt_ref=vt_ref, qbd=qbd, acc_ref=acc_ref, m_ref=m_ref,
                             tq=tq, nblk=nblk, blk_head=(0, 1), ck_ref=ck_ref, cq=cq)

    def body(j, carry):
        tile(j, masked=False)
        return carry

    lax.fori_loop(0, i, body, 0)
    tile(i, masked=True)
    outs = []
    for hh in range(nblk):
        a = acc_ref[hh]
        outs.append(a[0:HEAD_DIM] / a[HEAD_DIM:HEAD_DIM + 1])
    o_ref[0] = jnp.concatenate(outs, axis=0).T.astype(o_ref.dtype)


def _prompt_attention(qdt, kdb, vdt, qft, kfb, vft, ck_pairs, cqt_pairs, lambdas, gain128, lam_init, tq):
    nb, _, seq = qdt.shape
    n_pairs = DIFF_W // LANES
    grid = (nb, n_pairs, seq // tq)
    qt_spec = pl.BlockSpec((1, LANES, tq), lambda b, p, i: (b, p, i))
    k_spec = pl.BlockSpec((1, seq, LANES), lambda b, p, i: (b, 0, p))
    vt_spec = pl.BlockSpec((1, 2, seq // tq, V_ROWS, tq), lambda b, p, i: (b, p, 0, 0, 0))
    o_spec = pl.BlockSpec((1, tq, LANES), lambda b, p, i: (b, i, p))
    small = lambda a: pl.BlockSpec(a.shape, lambda b, p, i: (0,) * a.ndim)
    sem = ("parallel", "parallel", "arbitrary")
    od = pl.pallas_call(
        functools.partial(_diff_attn_kernel, tq=tq, lam_init=lam_init),
        grid=grid,
        in_specs=[small(lambdas[0])] * 4 + [small(gain128), qt_spec, k_spec, vt_spec],
        out_specs=o_spec,
        out_shape=jax.ShapeDtypeStruct((nb, seq, DIFF_W), BF16),
        scratch_shapes=[pltpu.VMEM((4, V_ROWS, tq), F32), pltpu.VMEM((4, SUBLANES, tq), F32)],
        compiler_params=_cparams(sem),
        name="diff_attention",
    )(*lambdas, gain128, qdt, kdb, vdt)
    ck_spec = pl.BlockSpec((1, 1, seq, 2), lambda b, p, i: (b, p, 0, 0))
    cq_spec = pl.BlockSpec((1, 1, 2, tq), lambda b, p, i: (b, p, 0, i))
    of = pl.pallas_call(
        functools.partial(_fox_attn_kernel, tq=tq),
        grid=grid,
        in_specs=[qt_spec, k_spec, vt_spec, ck_spec, cq_spec],
        out_specs=o_spec,
        out_shape=jax.ShapeDtypeStruct((nb, seq, FOX_W), BF16),
        scratch_shapes=[pltpu.VMEM((2, V_ROWS, tq), F32), pltpu.VMEM((2, SUBLANES, tq), F32)],
        compiler_params=_cparams(sem),
        name="fox_attention",
    )(qft, kfb, vft, ck_pairs, cqt_pairs)
    return od, of


def _row_softmax_update(s, v_bf, m_ref, l_ref, acc_ref):
    m_old = m_ref[:, 0:1]
    m_new = jnp.maximum(m_old, jnp.max(s, axis=1, keepdims=True))
    p = jnp.exp(s - m_new)
    alpha = jnp.exp(m_old - m_new)
    l_new = alpha * l_ref[:, 0:1] + jnp.sum(p, axis=1, keepdims=True)
    acc_ref[...] = alpha * acc_ref[...] + jnp.dot(p.astype(BF16), v_bf, preferred_element_type=F32)
    m_ref[...] = jnp.broadcast_to(m_new, m_ref.shape)
    l_ref[...] = jnp.broadcast_to(l_new, l_ref.shape)


def _sample_attn_kernel(pt_ref, lq1_ref, lk1_ref, lq2_ref, lk2_ref, gain_ref, qd_ref, qf_ref,
                        dk_ref, dv_ref, fk_ref, fv_ref, lf_ref, sdk_ref, sdv_ref, sfk_ref, sfv_ref, slf_ref,
                        od_ref, of_ref, md_ref, ld_ref, accd_ref, mf_ref, lfs_ref, accf_ref, carry_ref, cself_ref,
                        *, dec_seq, lam_init):
    j = pl.program_id(1)
    rd = dec_seq * 2 * N_DIFF_HEADS
    rf = dec_seq * N_FOX_HEADS
    lane_d = lax.broadcasted_iota(jnp.int32, (rd, LANES), 1)
    t_d = lax.broadcasted_iota(jnp.int32, (rd, LANES), 0) // (2 * N_DIFF_HEADS)
    lane_f = lax.broadcasted_iota(jnp.int32, (rf, LANES), 1)
    t_f = lax.broadcasted_iota(jnp.int32, (rf, LANES), 0) // N_FOX_HEADS

    def scores(q_ref, k):
        return lax.dot_general(q_ref[0], k.astype(BF16), NT_DIMS, preferred_element_type=F32)

    @pl.when(j == 0)
    def _():
        md_ref[...] = jnp.full(md_ref.shape, NEG_INF, F32)
        ld_ref[...] = jnp.zeros(ld_ref.shape, F32)
        accd_ref[...] = jnp.zeros(accd_ref.shape, F32)
        mf_ref[...] = jnp.full(mf_ref.shape, NEG_INF, F32)
        lfs_ref[...] = jnp.zeros(lfs_ref.shape, F32)
        accf_ref[...] = jnp.zeros(accf_ref.shape, F32)
        carry_ref[...] = jnp.zeros(carry_ref.shape, F32)
        sd = jnp.where((lane_d < dec_seq) & (lane_d <= t_d), scores(qd_ref, sdk_ref[0]), NEG_INF)
        _row_softmax_update(sd, sdv_ref[0].astype(BF16), md_ref, ld_ref, accd_ref)
        incl = _lane_cumsum(slf_ref[0])
        cs_t = jnp.sum(jnp.where(lane_f == t_f, incl, 0.0), axis=1, keepdims=True)
        cself_ref[...] = jnp.broadcast_to(cs_t, cself_ref.shape)
        sf = scores(qf_ref, sfk_ref[0]) + (cs_t - incl)
        sf = jnp.where((lane_f < dec_seq) & (lane_f <= t_f), sf, NEG_INF)
        _row_softmax_update(sf, sfv_ref[0].astype(BF16), mf_ref, lfs_ref, accf_ref)

    @pl.when(j > 0)
    def _():
        _row_softmax_update(scores(qd_ref, dk_ref[0]), dv_ref[0].astype(BF16), md_ref, ld_ref, accd_ref)
        lf = lf_ref[0]
        suffix = _lane_rev_cumsum(lf)
        bias = cself_ref[...] + carry_ref[...] + (suffix - lf)
        carry_ref[...] = carry_ref[...] + suffix[:, 0:1]
        _row_softmax_update(scores(qf_ref, fk_ref[0]) + bias, fv_ref[0].astype(BF16), mf_ref, lfs_ref, accf_ref)

    @pl.when(j == pl.num_programs(1) - 1)
    def _():
        lam = _lambda_full(lq1_ref, lk1_ref, lq2_ref, lk2_ref, lam_init)
        width = accd_ref.shape[1]
        accn = accd_ref[...] / ld_ref[:, 0:1]
        col_h = lax.broadcasted_iota(jnp.int32, (rd, width), 1) // HEAD_DIM
        row = lax.broadcasted_iota(jnp.int32, (rd, width), 0)
        accn = jnp.where(col_h == (row % (2 * N_DIFF_HEADS)) // 2, accn, 0.0)
        sr = lax.broadcasted_iota(jnp.int32, (BF16_ROWS, rd), 0)
        sc = lax.broadcasted_iota(jnp.int32, (BF16_ROWS, rd), 1)
        o_maps = []
        for mp in range(2):
            sel = ((sc // (2 * N_DIFF_HEADS) == sr) & (sc % 2 == mp)).astype(BF16)
            o_maps.append(_dot_sel_lhs(sel, accn))
        od = o_maps[0] - lam * o_maps[1]
        gi = lax.broadcasted_iota(jnp.int32, (width, width), 0) // HEAD_DIM
        gj = lax.broadcasted_iota(jnp.int32, (width, width), 1) // HEAD_DIM
        gmat = jnp.where(gi == gj, 1.0 / HEAD_DIM, 0.0).astype(BF16)
        ms = _dot_sel_rhs(od * od, gmat)
        od_ref[0] = (od * lax.rsqrt(ms + RMS_EPS) * gain_ref[...] * (1.0 - lam_init)).astype(od_ref.dtype)

        accfn = accf_ref[...] / lfs_ref[:, 0:1]
        col_hf = lax.broadcasted_iota(jnp.int32, (rf, width), 1) // HEAD_DIM
        row_f = lax.broadcasted_iota(jnp.int32, (rf, width), 0)
        accfn = jnp.where(col_hf == row_f % N_FOX_HEADS, accfn, 0.0)
        srf = lax.broadcasted_iota(jnp.int32, (BF16_ROWS, rf), 0)
        scf = lax.broadcasted_iota(jnp.int32, (BF16_ROWS, rf), 1)
        self_ = (scf // N_FOX_HEADS == srf).astype(BF16)
        of_ref[0] = _dot_sel_lhs(self_, accfn).astype(of_ref.dtype)


def _sample_attention(page_table, qd_bd, qf_bd, caches, selfs, lambdas, gain512, lam_init, dec_seq):
    nb, n_pages = page_table.shape
    page = caches[0].shape[1]
    assert page == LANES
    rd, rf = qd_bd.shape[1], qf_bd.shape[1]

    def page_map(b, j, pt):
        return (pt[b, n_pages - jnp.maximum(j, 1)], 0, 0)

    per_b = lambda b, j, pt: (b, 0, 0)
    small = lambda a: pl.BlockSpec(a.shape, lambda b, j, pt: (0,) * a.ndim)
    cache_spec = lambda a: pl.BlockSpec((1,) + a.shape[1:], page_map)
    b_spec = lambda a: pl.BlockSpec((1,) + a.shape[1:], per_b)
    out_spec = pl.BlockSpec((1, BF16_ROWS, DIFF_W), per_b)
    grid_spec = pltpu.PrefetchScalarGridSpec(
        num_scalar_prefetch=1,
        grid=(nb, n_pages + 1),
        in_specs=[small(lambdas[0])] * 4 + [small(gain512), b_spec(qd_bd), b_spec(qf_bd)]
                 + [cache_spec(a) for a in caches] + [b_spec(a) for a in selfs],
        out_specs=[out_spec, out_spec],
        scratch_shapes=[pltpu.VMEM((rd, LANES), F32), pltpu.VMEM((rd, LANES), F32), pltpu.VMEM((rd, DIFF_W), F32),
                        pltpu.VMEM((rf, LANES), F32), pltpu.VMEM((rf, LANES), F32), pltpu.VMEM((rf, FOX_W), F32),
                        pltpu.VMEM((rf, LANES), F32), pltpu.VMEM((rf, LANES), F32)],
    )
    return pl.pallas_call(
        functools.partial(_sample_attn_kernel, dec_seq=dec_seq, lam_init=lam_init),
        grid_spec=grid_spec,
        out_shape=[jax.ShapeDtypeStruct((nb, BF16_ROWS, DIFF_W), BF16)] * 2,
        compiler_params=_cparams(("parallel", "arbitrary")),
        name="sample_attention",
    )(page_table, *lambdas, gain512, qd_bd, qf_bd, *caches, *selfs)


def _merge_kernel(x_ref, od_ref, of_ref, wo_ref, g_ref, b_ref, wrt_ref, rb_ref, h_ref, hb_ref, gt_ref, *, alpha):
    o = jnp.concatenate([od_ref[...], of_ref[...]], axis=1)
    mix = jnp.dot(o, wo_ref[...], preferred_element_type=F32)
    h = _layer_norm(alpha * x_ref[...] + mix, g_ref[...], b_ref[...])
    h_ref[...] = h
    hb_ref[...] = h.astype(BF16)

    tm = h.shape[0]
    h1, h2, h3 = _split3(h)
    w1, w2, w3 = _split3(wrt_ref[...])
    nt = lambda a, b: lax.dot_general(a, b, NT_DIMS, preferred_element_type=F32)
    logits = (nt(w1, h1) + (nt(w1, h2) + nt(w2, h1)) + (nt(w1, h3) + nt(w2, h2) + nt(w3, h1)))
    scores = jax.nn.sigmoid(logits)
    choice = scores + rb_ref[...]
    member = lax.broadcasted_iota(jnp.int32, (GROUP_SIZE, tm), 0)
    blocks, gscore = [], []
    for g in range(N_EXPERT_GROUPS):
        blk = choice[g * GROUP_SIZE:(g + 1) * GROUP_SIZE]
        m1 = jnp.max(blk, axis=0, keepdims=True)
        first = jnp.min(jnp.where(blk == m1, member, GROUP_SIZE), axis=0, keepdims=True)
        m2 = jnp.max(jnp.where(member == first, -jnp.inf, blk), axis=0, keepdims=True)
        blocks.append(blk)
        gscore.append(m1 + m2)
    masked = []
    for g in range(N_EXPERT_GROUPS):
        rank = jnp.zeros((1, tm), jnp.int32)
        for g2 in range(N_EXPERT_GROUPS):
            if g2 == g:
                continue
            beats = (gscore[g2] > gscore[g]) | ((gscore[g2] == gscore[g]) & (g2 < g))
            rank = rank + beats.astype(jnp.int32)
        masked.append(jnp.where(rank < TOPK_GROUPS, blocks[g], NEG_INF))
    vm = jnp.concatenate(masked, axis=0)
    eidx = lax.broadcasted_iota(jnp.int32, (N_EXPERTS, tm), 0)
    cnt = jnp.zeros((N_EXPERTS, tm), jnp.int32)
    for e2 in range(N_EXPERTS):
        r = vm[e2:e2 + 1]
        beats = (r > vm) | ((r == vm) & (e2 < eidx))
        cnt = cnt + beats.astype(jnp.int32)
    w = jnp.where(cnt < TOP_K, scores, 0.0)
    gt_ref[...] = w / (jnp.sum(w, axis=0, keepdims=True) + 1e-20) * ROUTED_SCALE


def _merge(x2d, od, of, wo_bf, g, b, wrt, rb, alpha, tm):
    n, d = x2d.shape
    row = lambda w: pl.BlockSpec((tm, w), lambda r: (r, 0))
    full = lambda a: pl.BlockSpec(a.shape, lambda r: (0,) * a.ndim)
    return pl.pallas_call(
        functools.partial(_merge_kernel, alpha=alpha),
        grid=(n // tm,),
        in_specs=[row(d), row(DIFF_W), row(FOX_W), full(wo_bf), full(g), full(b), full(wrt), full(rb)],
        out_specs=[row(d), row(d), pl.BlockSpec((N_EXPERTS, tm), lambda r: (0, r))],
        out_shape=[jax.ShapeDtypeStruct((n, d), F32), jax.ShapeDtypeStruct((n, d), BF16),
                   jax.ShapeDtypeStruct((N_EXPERTS, n), F32)],
        compiler_params=_cparams(("parallel",)),
        name="merge_ln_router",
    )(x2d, od, of, wo_bf, g, b, wrt, rb)


def _swiglu(hb, wgu, f):
    au = jnp.dot(hb, wgu, preferred_element_type=F32)
    return jax.nn.silu(au[:, :f]) * au[:, f:]


def _moe_kernel(hb_ref, h_ref, g_ref, wgu_ref, wd_ref, wsgu_ref, wsd_ref, ln_g_ref, ln_b_ref, y_ref, acc_ref, *, alpha):
    e = pl.program_id(1)
    hb = hb_ref[...]
    f = wd_ref.shape[1]
    gates = g_ref[...]
    lane = lax.broadcasted_iota(jnp.int32, gates.shape, 1)
    gcol = jnp.sum(jnp.where(lane == e, gates, 0.0), axis=1, keepdims=True)
    act = (_swiglu(hb, wgu_ref[0], f) * gcol).astype(BF16)
    contrib = jnp.dot(act, wd_ref[0], preferred_element_type=F32)

    @pl.when(e == 0)
    def _():
        fs = wsd_ref.shape[0]
        shared = jnp.dot(_swiglu(hb, wsgu_ref[...], fs).astype(BF16), wsd_ref[...], preferred_element_type=F32)
        acc_ref[...] = contrib + shared

    @pl.when(e > 0)
    def _():
        acc_ref[...] += contrib

    @pl.when(e == pl.num_programs(1) - 1)
    def _():
        y_ref[...] = _layer_norm(alpha * h_ref[...] + acc_ref[...], ln_g_ref[...], ln_b_ref[...])


def _moe(hb, h, gates, wgu, wd, wsgu, wsd, ln_g, ln_b, alpha, tn):
    n, d = h.shape
    ne = wgu.shape[0]
    row = lambda w: pl.BlockSpec((tn, w), lambda i, e: (i, 0))
    full = lambda a: pl.BlockSpec(a.shape, lambda i, e: (0,) * a.ndim)
    return pl.pallas_call(
        functools.partial(_moe_kernel, alpha=alpha),
        grid=(n // tn, ne),
        in_specs=[row(d), row(d), row(ne),
                  pl.BlockSpec((1,) + wgu.shape[1:], lambda i, e: (e, 0, 0)),
                  pl.BlockSpec((1,) + wd.shape[1:], lambda i, e: (e, 0, 0)),
                  full(wsgu), full(wsd), full(ln_g), full(ln_b)],
        out_specs=row(d),
        out_shape=jax.ShapeDtypeStruct((n, d), F32),
        scratch_shapes=[pltpu.VMEM((tn, d), F32)],
        compiler_params=_cparams(("parallel", "arbitrary")),
        name="moe_ln",
    )(hb, h, gates, wgu, wd, wsgu, wsd, ln_g, ln_b)


def _pick_tile(n, pref):
    t = min(n, pref)
    assert n % t == 0
    return t


def kernel(x_prompt, x_sample, cache_diff_k, cache_diff_v, cache_fox_k, cache_fox_v, cache_fox_logf, page_table,
           w_in, b_forget, lambda_q1, lambda_k1, lambda_q2, lambda_k2, subln_gain, w_o, ln1_g, ln1_b, w_router,
           router_bias, w_exp_gate, w_exp_up, w_exp_down, w_sh_gate, w_sh_up, w_sh_down, ln2_g, ln2_b):
    depth = w_in.shape[0]
    nb, seq, d_model = x_prompt.shape
    dec_b, dec_seq, _ = x_sample.shape
    n_pages = page_table.shape[1]
    page = cache_diff_k.shape[2]
    past_len = n_pages * page
    alpha = (2.0 * depth) ** 0.25
    n_s = dec_b * dec_seq

    tq = _pick_tile(seq, 256)
    tm_p = _pick_tile(seq, 512)
    tabs_p = _rope_tables(seq, 0, seq)
    tabs_s = _rope_tables(n_s, past_len, dec_seq)

    xp = x_prompt.reshape(nb * seq, d_model)
    xs = x_sample.reshape(n_s, d_model)
    outs_p, outs_s = [], []
    for l in range(depth):
        lam_init = 0.8 - 0.6 * math.exp(-0.3 * l)
        w = w_in[l]
        cuts = [0, DIFF_W, 2 * DIFF_W, 3 * DIFF_W, 3 * DIFF_W + FOX_W, 3 * DIFF_W + 2 * FOX_W, 3 * DIFF_W + 3 * FOX_W]
        wqd, wkd, wvd, wqf, wkf, wvf = [w[:, cuts[i]:cuts[i + 1]] for i in range(6)]
        wfl = w[:, cuts[6]:]
        pw = {
            "wkv": jnp.concatenate([wkd, wvd, wkf, wvf], axis=1).astype(BF16),
            "wfl": jnp.pad(wfl, ((0, 0), (0, LANES - N_FOX_HEADS))).astype(BF16),
            "wtq": jnp.concatenate([wqd, wqf], axis=1).T.astype(BF16),
            "wtv": jnp.concatenate([wvd, wvf], axis=1).T.astype(BF16),
            "wtfl": jnp.pad(wfl.T, ((0, BF16_ROWS - N_FOX_HEADS), (0, 0))).astype(BF16),
            "bfl": jnp.pad(b_forget[l][None, :], ((0, 0), (0, LANES - N_FOX_HEADS))),
            "btfl": b_forget[l][:, None],
        }
        lambdas = [v[l][None, :] for v in (lambda_q1, lambda_k1, lambda_q2, lambda_k2)]
        gain = subln_gain[l]
        gain128 = jnp.tile(gain, LANES // HEAD_DIM)[None, :]
        gain512 = jnp.tile(gain, DIFF_W // HEAD_DIM)[None, :]
        wo_bf = w_o[l].astype(BF16)
        wrt = w_router[l].T
        rb = router_bias[l][:, None]
        wgu = jnp.concatenate([w_exp_gate[l], w_exp_up[l]], axis=2).astype(BF16)
        wd = w_exp_down[l].astype(BF16)
        wsgu = jnp.concatenate([w_sh_gate[l], w_sh_up[l]], axis=1).astype(BF16)
        wsd = w_sh_down[l].astype(BF16)
        g1, b1 = ln1_g[l][None, :], ln1_b[l][None, :]
        g2, b2 = ln2_g[l][None, :], ln2_b[l][None, :]

        def ffn(x2d, od, of, tm, tn):
            h, hb, gt = _merge(x2d, od, of, wo_bf, g1, b1, wrt, rb, alpha, tm)
            return _moe(hb, h, gt.T, wgu, wd, wsgu, wsd, g2, b2, alpha, tn)

        (kd, vd, kf, vf, logf, kdb, kfb, qdt, qft, vdt, vft, logft) = _project(xp, nb, seq, pw, tabs_p, tm_p, tq)
        ct = _cumsum_t(logft)
        n_pairs = N_FOX_HEADS // 2
        cqt_pairs = ct.reshape(nb, n_pairs, 2, seq)
        ck_pairs = jnp.transpose(cqt_pairs, (0, 1, 3, 2))
        od, of = _prompt_attention(qdt, kdb.reshape(nb, seq, DIFF_W), vdt, qft, kfb.reshape(nb, seq, FOX_W), vft,
                                   ck_pairs, cqt_pairs, lambdas, gain128, lam_init, tq)
        xp_new = ffn(xp, od.reshape(nb * seq, DIFF_W), of.reshape(nb * seq, FOX_W),
                     _pick_tile(nb * seq, 256), _pick_tile(nb * seq, 1024))
        outs_p.append((kd, vd, kf, vf, logf))

        (skd, svd, skf, svf, slogf, _, _, sqdt, sqft, _, _, _) = _project(xs, 1, n_s, pw, tabs_s, n_s, n_s)
        qd_rows = sqdt[0].T.reshape(dec_b, dec_seq, 2 * N_DIFF_HEADS, DIFF_D)
        eye_hm = jnp.eye(2 * N_DIFF_HEADS, dtype=BF16)
        qd_bd = (qd_rows[:, :, :, None, :] * eye_hm[None, None, :, :, None]).reshape(
            dec_b, dec_seq * 2 * N_DIFF_HEADS, DIFF_W)
        qf_rows = sqft[0].T.reshape(dec_b, dec_seq, N_FOX_HEADS, HEAD_DIM)
        eye_h = jnp.eye(N_FOX_HEADS, dtype=BF16)
        qf_bd = (qf_rows[:, :, :, None, :] * eye_h[None, None, :, :, None]).reshape(
            dec_b, dec_seq * N_FOX_HEADS, FOX_W)
        n_pool = cache_diff_k.shape[1]
        caches = [cache_diff_k[l].reshape(n_pool, page, DIFF_W), cache_diff_v[l].reshape(n_pool, page, DIFF_W),
                  cache_fox_k[l].reshape(n_pool, page, FOX_W), cache_fox_v[l].reshape(n_pool, page, FOX_W),
                  jnp.tile(jnp.transpose(cache_fox_logf[l], (0, 2, 1)), (1, dec_seq, 1))]
        pad_rows = lambda a: jnp.pad(a.reshape(dec_b, dec_seq, -1), ((0, 0), (0, page - dec_seq), (0, 0)))
        slf_t = jnp.transpose(slogf.reshape(dec_b, dec_seq, N_FOX_HEADS), (0, 2, 1))
        slf_t = jnp.pad(jnp.tile(slf_t, (1, dec_seq, 1)), ((0, 0), (0, 0), (0, page - dec_seq)))
        selfs = [pad_rows(skd), pad_rows(svd), pad_rows(skf), pad_rows(svf), slf_t]
        od_s, of_s = _sample_attention(page_table, qd_bd, qf_bd, caches, selfs, lambdas, gain512, lam_init, dec_seq)
        od_s = od_s[:, :dec_seq].reshape(n_s, DIFF_W)
        of_s = of_s[:, :dec_seq].reshape(n_s, FOX_W)
        xs_new = ffn(xs, od_s, of_s, n_s, n_s)
        outs_s.append((skd, svd, skf, svf, slogf))
        xp, xs = xp_new, xs_new

    def stack(outs, b, t):
        kd = jnp.stack([o[0] for o in outs]).reshape(depth, b, t, N_DIFF_HEADS, 2, DIFF_D)
        vd = jnp.stack([o[1] for o in outs]).reshape(depth, b, t, N_DIFF_HEADS, HEAD_DIM)
        kf = jnp.stack([o[2] for o in outs]).reshape(depth, b, t, N_FOX_HEADS, HEAD_DIM)
        vf = jnp.stack([o[3] for o in outs]).reshape(depth, b, t, N_FOX_HEADS, HEAD_DIM)
        lf = jnp.stack([o[4] for o in outs]).reshape(depth, b, t, N_FOX_HEADS)
        return kd, vd, kf, vf, lf

    return (xp.reshape(nb, seq, d_model), xs.reshape(dec_b, dec_seq, d_model),
            *stack(outs_p, nb, seq), *stack(outs_s, dec_b, dec_seq))
```

```python
import functools
import math

import jax
import jax.numpy as jnp
from jax import lax
from jax.experimental import pallas as pl
from jax.experimental.pallas import tpu as pltpu

F32 = jnp.float32
BF16 = jnp.bfloat16

HEAD_DIM = 64
N_DIFF_HEADS = 8
N_FOX_HEADS = 8
DIFF_D = HEAD_DIM // 2
DIFF_W = N_DIFF_HEADS * HEAD_DIM
FOX_W = N_FOX_HEADS * HEAD_DIM
ROT_DIM = DIFF_D // 4
ROT_HALF = ROT_DIM // 2
ROPE_THETA = 500000.0
N_EXPERTS = 64
N_EXPERT_GROUPS = 8
GROUP_SIZE = N_EXPERTS // N_EXPERT_GROUPS
TOPK_GROUPS = 4
TOP_K = 8
ROUTED_SCALE = 2.5
LN_EPS = 1e-5
RMS_EPS = 1e-5
NEG_INF = -1e30
LOG2E = math.log2(math.e)

LANES = 128
SUBLANES = 8
BF16_ROWS = 16
V_ROWS = HEAD_DIM + BF16_ROWS
VMEM_LIMIT = 56 * 1024 * 1024

NT_DIMS = (((1,), (1,)), ((), ()))


def _cparams(sem):
    return pltpu.CompilerParams(dimension_semantics=sem, vmem_limit_bytes=VMEM_LIMIT)


def _log_sigmoid(x):
    return jnp.minimum(x, 0.0) - jnp.log1p(jnp.exp(-jnp.abs(x)))


def _layer_norm(y, g, b):
    mu = jnp.mean(y, axis=-1, keepdims=True)
    d = y - mu
    var = jnp.mean(d * d, axis=-1, keepdims=True)
    return d * lax.rsqrt(var + LN_EPS) * g + b


def _split3(a):
    a1 = a.astype(BF16)
    r1 = a - a1.astype(F32)
    a2 = r1.astype(BF16)
    a3 = (r1 - a2.astype(F32)).astype(BF16)
    return a1, a2, a3


def _dot_sel_lhs(sel, b):
    return sum(jnp.dot(sel, t, preferred_element_type=F32) for t in _split3(b))


def _dot_sel_rhs(a, sel):
    return sum(jnp.dot(t, sel, preferred_element_type=F32) for t in _split3(a))


def _lambda_full(lq1_ref, lk1_ref, lq2_ref, lk2_ref, lam_init):
    a = jnp.sum(lq1_ref[...] * lk1_ref[...], axis=1, keepdims=True)
    b = jnp.sum(lq2_ref[...] * lk2_ref[...], axis=1, keepdims=True)
    return jnp.exp(a) - jnp.exp(b) + lam_init


def _rope_table_kernel(invf_lane_ref, invf_sub_ref, c_ref, s1_ref, s2_ref, ct_ref, st_ref, *, base, mod, tm):
    i = pl.program_id(0)
    row = lax.broadcasted_iota(jnp.int32, (tm, LANES), 0) + i * tm
    pos = (base + row % mod).astype(F32)
    ang = pos * invf_lane_ref[...]
    li = lax.broadcasted_iota(jnp.int32, (tm, LANES), 1) % DIFF_D
    c = jnp.cos(ang)
    s = jnp.sin(ang)
    c_ref[...] = jnp.where(li < ROT_DIM, c, 1.0)
    s1_ref[...] = jnp.where(li < ROT_HALF, -s, 0.0)
    s2_ref[...] = jnp.where((li >= ROT_HALF) & (li < ROT_DIM), s, 0.0)
    col = lax.broadcasted_iota(jnp.int32, (SUBLANES, tm), 1) + i * tm
    pos_t = (base + col % mod).astype(F32)
    ang_t = invf_sub_ref[...] * pos_t
    sub = lax.broadcasted_iota(jnp.int32, (SUBLANES, tm), 0)
    ct_ref[...] = jnp.cos(ang_t)
    st_ref[...] = jnp.where(sub < ROT_HALF, -jnp.sin(ang_t), jnp.sin(ang_t))


def _rope_tables(n_pos, base, mod):
    inv4 = jnp.power(ROPE_THETA, -jnp.arange(ROT_HALF, dtype=F32) * 2.0 / ROT_DIM)
    invf_lane = inv4[(jnp.arange(LANES) % DIFF_D) % ROT_HALF][None, :]
    invf_sub = inv4[jnp.arange(SUBLANES) % ROT_HALF][:, None]
    tm = min(n_pos, 1024)
    assert n_pos % tm == 0
    lane_spec = pl.BlockSpec((tm, LANES), lambda i: (i, 0))
    sub_spec = pl.BlockSpec((SUBLANES, tm), lambda i: (0, i))
    return pl.pallas_call(
        functools.partial(_rope_table_kernel, base=base, mod=mod, tm=tm),
        grid=(n_pos // tm,),
        in_specs=[pl.BlockSpec((1, LANES), lambda i: (0, 0)), pl.BlockSpec((SUBLANES, 1), lambda i: (0, 0))],
        out_specs=[lane_spec, lane_spec, lane_spec, sub_spec, sub_spec],
        out_shape=[jax.ShapeDtypeStruct((n_pos, LANES), F32)] * 3 + [jax.ShapeDtypeStruct((SUBLANES, n_pos), F32)] * 2,
        compiler_params=_cparams(("parallel",)),
        name="rope_tables",
    )(invf_lane, invf_sub)


def _rope_sublanes(zt, ct, st):
    rows = []
    for j in range(DIFF_W // DIFF_D):
        x8 = zt[DIFF_D * j:DIFF_D * j + ROT_DIM]
        rows.append(x8 * ct + pltpu.roll(x8, ROT_HALF, 0) * st)
        rows.append(zt[DIFF_D * j + ROT_DIM:DIFF_D * (j + 1)])
    return jnp.concatenate(rows, axis=0)


def _proj_kernel(x_ref, wt_ref, wk_ref, wtfl_ref, btfl_ref, c_ref, s1_ref, s2_ref, ct_ref, st_ref,
                 qdt_ref, qft_ref, kdt_ref, kft_ref, vdt_ref, vft_ref, kdb_ref, kfb_ref, v1d_ref, v1f_ref, logft_ref,
                 *, tm, tk):
    xb = x_ref[...].astype(BF16)
    zt = lax.dot_general(wt_ref[...], xb, NT_DIMS, preferred_element_type=F32)
    ct, st = ct_ref[...], st_ref[...]
    w = DIFF_W
    qdt_ref[0] = (_rope_sublanes(zt[0:w], ct, st) * (DIFF_D ** -0.5 * LOG2E)).astype(BF16)
    qft_ref[0] = (zt[w:2 * w] * (HEAD_DIM ** -0.5 * LOG2E)).astype(BF16)
    kdt_ref[0] = _rope_sublanes(zt[2 * w:3 * w], ct, st)
    kft_ref[0] = zt[3 * w:4 * w]
    vd = zt[4 * w:5 * w]
    vf = zt[5 * w:6 * w]
    vdt_ref[0] = vd
    vft_ref[0] = vf
    ones = jnp.ones((BF16_ROWS, tk), BF16)
    for h in range(N_DIFF_HEADS):
        for t in range(tm // tk):
            v1d_ref[0, h, t, 0:HEAD_DIM, :] = vd[HEAD_DIM * h:HEAD_DIM * (h + 1), t * tk:(t + 1) * tk].astype(BF16)
            v1d_ref[0, h, t, HEAD_DIM:V_ROWS, :] = ones
            v1f_ref[0, h, t, 0:HEAD_DIM, :] = vf[HEAD_DIM * h:HEAD_DIM * (h + 1), t * tk:(t + 1) * tk].astype(BF16)
            v1f_ref[0, h, t, HEAD_DIM:V_ROWS, :] = ones
    flt = lax.dot_general(wtfl_ref[...], xb, NT_DIMS, preferred_element_type=F32)
    logft_ref[0] = _log_sigmoid(flt[:N_FOX_HEADS] + btfl_ref[...])

    zk = jnp.dot(xb, wk_ref[...], preferred_element_type=F32)
    c, s1, s2 = c_ref[...], s1_ref[...], s2_ref[...]
    pieces = []
    for j in range(DIFF_W // LANES):
        xk = zk[:, j * LANES:(j + 1) * LANES]
        pieces.append(xk * c + pltpu.roll(xk, LANES - ROT_HALF, 1) * s1 + pltpu.roll(xk, ROT_HALF, 1) * s2)
    kdb_ref[...] = jnp.concatenate(pieces, axis=1).astype(BF16)
    kfb_ref[...] = zk[:, DIFF_W:].astype(BF16)


def _project(x2d, n_batch, seq, w, tables, tm, tk):
    n, d_model = x2d.shape
    nt = seq // tm
    c, s1, s2, ct, st = tables
    full = lambda a: pl.BlockSpec(a.shape, lambda r: (0,) * a.ndim)
    row512 = pl.BlockSpec((tm, DIFF_W), lambda r: (r, 0))
    tab = pl.BlockSpec((tm, LANES), lambda r: (r % nt, 0))
    tabt = pl.BlockSpec((SUBLANES, tm), lambda r: (0, r % nt))
    ft_spec = pl.BlockSpec((1, DIFF_W, tm), lambda r: (r // nt, 0, r % nt))
    v1_spec = pl.BlockSpec((1, N_DIFF_HEADS, tm // tk, V_ROWS, tk), lambda r: (r // nt, 0, r % nt, 0, 0))
    lt_spec = pl.BlockSpec((1, N_FOX_HEADS, tm), lambda r: (r // nt, 0, r % nt))
    sds = jax.ShapeDtypeStruct
    return pl.pallas_call(
        functools.partial(_proj_kernel, tm=tm, tk=tk),
        grid=(n // tm,),
        in_specs=[pl.BlockSpec((tm, d_model), lambda r: (r, 0)),
                  full(w["wt"]), full(w["wk"]), full(w["wtfl"]), full(w["btfl"]), tab, tab, tab, tabt, tabt],
        out_specs=[ft_spec] * 6 + [row512, row512, v1_spec, v1_spec, lt_spec],
        out_shape=[sds((n_batch, DIFF_W, seq), BF16)] * 2 + [sds((n_batch, DIFF_W, seq), F32)] * 4
                  + [sds((n, DIFF_W), BF16)] * 2
                  + [sds((n_batch, N_DIFF_HEADS, seq // tk, V_ROWS, tk), BF16)] * 2
                  + [sds((n_batch, N_FOX_HEADS, seq), F32)],
        compiler_params=_cparams(("parallel",)),
        name="in_proj",
    )(x2d, w["wt"], w["wk"], w["wtfl"], w["btfl"], c, s1, s2, ct, st)


def _lane_cumsum(x):
    lane = lax.broadcasted_iota(jnp.int32, x.shape, 1)
    s = 1
    while s < LANES:
        x = x + jnp.where(lane >= s, pltpu.roll(x, s, 1), 0.0)
        s *= 2
    return x


def _lane_rev_cumsum(x):
    lane = lax.broadcasted_iota(jnp.int32, x.shape, 1)
    s = 1
    while s < LANES:
        x = x + jnp.where(lane < LANES - s, pltpu.roll(x, LANES - s, 1), 0.0)
        s *= 2
    return x


def _cumsum_kernel(lft_ref, ct_ref, terms_ref, *, seq):
    carry = jnp.zeros((N_FOX_HEADS, 1), F32)
    for t in range(seq // LANES):
        x = _lane_cumsum(lft_ref[0, :, t * LANES:(t + 1) * LANES]) + carry
        ct_ref[0, :, t * LANES:(t + 1) * LANES] = x
        for n, term in enumerate(_split3(-LOG2E * x)):
            terms_ref[0, n, :, t * LANES:(t + 1) * LANES] = term.astype(F32)
        carry = x[:, LANES - 1:LANES]


def _cumsum_t(logft):
    nb, nh, seq = logft.shape
    spec = pl.BlockSpec((1, nh, seq), lambda b: (b, 0, 0))
    return pl.pallas_call(
        functools.partial(_cumsum_kernel, seq=seq),
        grid=(nb,), in_specs=[spec],
        out_specs=[spec, pl.BlockSpec((1, 3, nh, seq), lambda b: (b, 0, 0, 0))],
        out_shape=[jax.ShapeDtypeStruct(logft.shape, F32), jax.ShapeDtypeStruct((nb, 3, nh, seq), F32)],
        compiler_params=_cparams(("parallel",)),
        name="logf_cumsum",
    )(logft)


def _block_diag_q(qt, nblk):
    rows_per = LANES // nblk
    rb = lax.broadcasted_iota(jnp.int32, qt.shape, 0) // rows_per
    zero = jnp.zeros_like(qt)
    return jnp.concatenate([jnp.where(rb == j, qt, zero) for j in range(nblk)], axis=1)


def _scores(j, k_ref, qbd_ref, s_ref, tk, kbias_ref=None):
    start = pl.multiple_of(j * tk, tk)
    k = k_ref[0, pl.ds(start, tk), :]
    if kbias_ref is not None:
        k = jnp.concatenate([k, kbias_ref[0, 0, pl.ds(start, tk), :]], axis=1)
    s_ref[...] = jnp.dot(k, qbd_ref[...], preferred_element_type=F32)


def _softmax_pv(j, s_ref, vt_ref, acc_ref, m_ref, *, tq, nblk, blk_head, masked, cq=None):
    tk = tq
    if masked:
        keep = (lax.broadcasted_iota(jnp.int32, (tk, tq), 0) <= lax.broadcasted_iota(jnp.int32, (tk, tq), 1))
    for blk in range(nblk):
        sb = s_ref[:, blk * tq:(blk + 1) * tq]
        if masked:
            sb = jnp.where(keep, sb, NEG_INF)
        m_old = m_ref[blk, 0:1, :]
        smax = jnp.max(sb, axis=0, keepdims=True)
        if cq is not None:
            m_new = jnp.maximum(m_old, smax + cq[blk])
            shift = m_new - cq[blk]
        else:
            m_new = jnp.maximum(m_old, smax)
            shift = m_new
        p = jnp.exp2(sb - shift).astype(BF16)
        alpha = jnp.exp2(m_old - m_new)
        vt = vt_ref[0, blk_head[blk], j]
        acc_ref[blk] = acc_ref[blk] * alpha + jnp.dot(vt, p, preferred_element_type=F32)
        m_ref[blk] = jnp.broadcast_to(m_new, (SUBLANES, tq))


def _causal_sweep(i, scores, update):
    scores(0, 0)

    def body(jj, carry):
        j = 2 * jj
        scores(j + 1, 1)
        update(j, 0, False)
        scores(j + 2, 0)
        update(j + 1, 1, False)
        return carry

    lax.fori_loop(0, i // 2, body, 0)

    @pl.when(i % 2 == 0)
    def _():
        update(i, 0, True)

    @pl.when(i % 2 == 1)
    def _():
        scores(i, 1)
        update(i - 1, 0, False)
        update(i, 1, True)


def _attn_init(acc_ref, m_ref):
    acc_ref[...] = jnp.zeros(acc_ref.shape, F32)
    m_ref[...] = jnp.full(m_ref.shape, NEG_INF, F32)


def _diff_attn_kernel(lq1_ref, lk1_ref, lq2_ref, lk2_ref, gain_ref, qt_ref, k_ref, vt_ref, o_ref,
                      acc_ref, m_ref, qbd_ref, s0_ref, s1_ref, *, tq, lam_init):
    i = pl.program_id(2)
    nblk = 4
    qbd_ref[...] = _block_diag_q(qt_ref[0], nblk)
    _attn_init(acc_ref, m_ref)
    s_refs = (s0_ref, s1_ref)
    scores = lambda j, slot: _scores(j, k_ref, qbd_ref, s_refs[slot], tq)
    update = lambda j, slot, masked: _softmax_pv(j, s_refs[slot], vt_ref, acc_ref, m_ref, tq=tq, nblk=nblk,
                                                 blk_head=(0, 0, 1, 1), masked=masked)
    _causal_sweep(i, scores, update)

    lam = _lambda_full(lq1_ref, lk1_ref, lq2_ref, lk2_ref, lam_init)
    outs = []
    for hh in range(2):
        a1 = acc_ref[2 * hh]
        a2 = acc_ref[2 * hh + 1]
        o = a1[0:HEAD_DIM] / a1[HEAD_DIM:HEAD_DIM + 1] - lam * (a2[0:HEAD_DIM] / a2[HEAD_DIM:HEAD_DIM + 1])
        ms = jnp.mean(o * o, axis=0, keepdims=True)
        outs.append(o * lax.rsqrt(ms + RMS_EPS))
    o = jnp.concatenate(outs, axis=0).T
    o_ref[0] = (o * gain_ref[...] * (1.0 - lam_init)).astype(o_ref.dtype)


def _fox_attn_kernel(qt_ref, k_ref, vt_ref, kbias_ref, cqt_ref, o_ref, acc_ref, m_ref, qbd_ref, s0_ref, s1_ref, *, tq):
    i = pl.program_id(2)
    nblk = 2
    qbd_ref[0:LANES, :] = _block_diag_q(qt_ref[0], nblk)
    r = lax.broadcasted_iota(jnp.int32, (LANES, nblk * tq), 0)
    cb = lax.broadcasted_iota(jnp.int32, (LANES, nblk * tq), 1) // tq
    qbd_ref[LANES:2 * LANES, :] = jnp.where((r < 3 * nblk) & (r // 3 == cb), 1.0, 0.0).astype(BF16)
    _attn_init(acc_ref, m_ref)
    cq = [cqt_ref[0, 0, hh:hh + 1, :] * LOG2E for hh in range(nblk)]
    s_refs = (s0_ref, s1_ref)
    scores = lambda j, slot: _scores(j, k_ref, qbd_ref, s_refs[slot], tq, kbias_ref)
    update = lambda j, slot, masked: _softmax_pv(j, s_refs[slot], vt_ref, acc_ref, m_ref, tq=tq, nblk=nblk,
                                                 blk_head=(0, 1), masked=masked, cq=cq)
    _causal_sweep(i, scores, update)
    outs = []
    for hh in range(nblk):
        a = acc_ref[hh]
        outs.append(a[0:HEAD_DIM] / a[HEAD_DIM:HEAD_DIM + 1])
    o_ref[0] = jnp.concatenate(outs, axis=0).T.astype(o_ref.dtype)


def _prompt_attention(qdt, kdb, v1d, qft, kfb, v1f, kbias_pairs, cqt_pairs, lambdas, gain128, lam_init, tq):
    nb, _, seq = qdt.shape
    n_pairs = DIFF_W // LANES
    grid = (nb, n_pairs, seq // tq)
    qt_spec = pl.BlockSpec((1, LANES, tq), lambda b, p, i: (b, p, i))
    k_spec = pl.BlockSpec((1, seq, LANES), lambda b, p, i: (b, 0, p))
    vt_spec = pl.BlockSpec((1, 2, seq // tq, V_ROWS, tq), lambda b, p, i: (b, p, 0, 0, 0))
    o_spec = pl.BlockSpec((1, tq, LANES), lambda b, p, i: (b, i, p))
    small = lambda a: pl.BlockSpec(a.shape, lambda b, p, i: (0,) * a.ndim)
    sem = ("parallel", "parallel", "arbitrary")

    def scratch(nblk, k_width):
        return [pltpu.VMEM((nblk, V_ROWS, tq), F32), pltpu.VMEM((nblk, SUBLANES, tq), F32),
                pltpu.VMEM((k_width, nblk * tq), BF16),
                pltpu.VMEM((tq, nblk * tq), F32), pltpu.VMEM((tq, nblk * tq), F32)]

    od = pl.pallas_call(
        functools.partial(_diff_attn_kernel, tq=tq, lam_init=lam_init),
        grid=grid,
        in_specs=[small(lambdas[0])] * 4 + [small(gain128), qt_spec, k_spec, vt_spec],
        out_specs=o_spec,
        out_shape=jax.ShapeDtypeStruct((nb, seq, DIFF_W), BF16),
        scratch_shapes=scratch(4, LANES),
        compiler_params=_cparams(sem),
        name="diff_attention",
    )(*lambdas, gain128, qdt, kdb, v1d)
    kbias_spec = pl.BlockSpec((1, 1, seq, LANES), lambda b, p, i: (b, p, 0, 0))
    cq_spec = pl.BlockSpec((1, 1, 2, tq), lambda b, p, i: (b, p, 0, i))
    of = pl.pallas_call(
        functools.partial(_fox_attn_kernel, tq=tq),
        grid=grid,
        in_specs=[qt_spec, k_spec, vt_spec, kbias_spec, cq_spec],
        out_specs=o_spec,
        out_shape=jax.ShapeDtypeStruct((nb, seq, FOX_W), BF16),
        scratch_shapes=scratch(2, 2 * LANES),
        compiler_params=_cparams(sem),
        name="fox_attention",
    )(qft, kfb, v1f, kbias_pairs, cqt_pairs)
    return od, of


def _row_softmax_update(s, vts, m_ref, l_ref, acc_ref):
    m_old = m_ref[:, 0:1]
    m_new = jnp.maximum(m_old, jnp.max(s, axis=1, keepdims=True))
    p = jnp.exp2(s - m_new)
    alpha = jnp.exp2(m_old - m_new)
    l_new = alpha * l_ref[:, 0:1] + jnp.sum(p, axis=1, keepdims=True)
    pb = p.astype(BF16)
    pv = sum(lax.dot_general(pb[:, i * LANES:(i + 1) * LANES], vt.astype(BF16), NT_DIMS, preferred_element_type=F32)
             for i, vt in enumerate(vts))
    acc_ref[...] = alpha * acc_ref[...] + pv
    m_ref[...] = jnp.broadcast_to(m_new, m_ref.shape)
    l_ref[...] = jnp.broadcast_to(l_new, l_ref.shape)


def _sample_attn_kernel(pt_ref, lq1_ref, lk1_ref, lq2_ref, lk2_ref, gain_ref, qd_ref, qf_ref, *rest,
                        dec_seq, lam_init, pages_per_step):
    pp = pages_per_step
    dk_refs, dv_refs, fk_refs, fv_refs, lf_refs = (rest[i * pp:(i + 1) * pp] for i in range(5))
    (sdk_ref, sdv_ref, sfk_ref, sfv_ref, slf_ref, od_ref, of_ref,
     md_ref, ld_ref, accd_ref, mf_ref, lfs_ref, accf_ref, carry_ref, cself_ref) = rest[5 * pp:]
    j = pl.program_id(1)
    rd = dec_seq * 2 * N_DIFF_HEADS
    rf = dec_seq * N_FOX_HEADS
    lane_d = lax.broadcasted_iota(jnp.int32, (rd, LANES), 1)
    t_d = lax.broadcasted_iota(jnp.int32, (rd, LANES), 0) // (2 * N_DIFF_HEADS)
    lane_f = lax.broadcasted_iota(jnp.int32, (rf, LANES), 1)
    t_f = lax.broadcasted_iota(jnp.int32, (rf, LANES), 0) // N_FOX_HEADS

    def scores(q_ref, kts):
        q = q_ref[0]
        return jnp.concatenate([jnp.dot(q, kt.astype(BF16), preferred_element_type=F32) for kt in kts], axis=1)

    @pl.when(j == 0)
    def _():
        md_ref[...] = jnp.full(md_ref.shape, NEG_INF, F32)
        ld_ref[...] = jnp.zeros(ld_ref.shape, F32)
        accd_ref[...] = jnp.zeros(accd_ref.shape, F32)
        mf_ref[...] = jnp.full(mf_ref.shape, NEG_INF, F32)
        lfs_ref[...] = jnp.zeros(lfs_ref.shape, F32)
        accf_ref[...] = jnp.zeros(accf_ref.shape, F32)
        carry_ref[...] = jnp.zeros(carry_ref.shape, F32)
        sd = jnp.where((lane_d < dec_seq) & (lane_d <= t_d), scores(qd_ref, [sdk_ref[0]]), NEG_INF)
        _row_softmax_update(sd, [sdv_ref[0]], md_ref, ld_ref, accd_ref)
        incl = _lane_cumsum(slf_ref[0])
        cs_t = jnp.sum(jnp.where(lane_f == t_f, incl, 0.0), axis=1, keepdims=True)
        cself_ref[...] = jnp.broadcast_to(cs_t, cself_ref.shape)
        sf = scores(qf_ref, [sfk_ref[0]]) + (cs_t - incl) * LOG2E
        sf = jnp.where((lane_f < dec_seq) & (lane_f <= t_f), sf, NEG_INF)
        _row_softmax_update(sf, [sfv_ref[0]], mf_ref, lfs_ref, accf_ref)

    _row_softmax_update(scores(qd_ref, [r[0] for r in dk_refs]), [r[0] for r in dv_refs], md_ref, ld_ref, accd_ref)
    cself = cself_ref[:, 0:1]
    carry = carry_ref[:, 0:1]
    biases = []
    for r in lf_refs:
        lf = jnp.concatenate([r[0]] * dec_seq, axis=0)
        suffix = _lane_rev_cumsum(lf)
        biases.append(cself + carry + (suffix - lf))
        carry = carry + suffix[:, 0:1]
    carry_ref[...] = jnp.broadcast_to(carry, carry_ref.shape)
    sf = scores(qf_ref, [r[0] for r in fk_refs]) + jnp.concatenate(biases, axis=1) * LOG2E
    _row_softmax_update(sf, [r[0] for r in fv_refs], mf_ref, lfs_ref, accf_ref)

    @pl.when(j == pl.num_programs(1) - 1)
    def _():
        lam = _lambda_full(lq1_ref, lk1_ref, lq2_ref, lk2_ref, lam_init)
        width = accd_ref.shape[1]
        accn = accd_ref[...] / ld_ref[:, 0:1]
        col_h = lax.broadcasted_iota(jnp.int32, (rd, width), 1) // HEAD_DIM
        row = lax.broadcasted_iota(jnp.int32, (rd, width), 0)
        accn = jnp.where(col_h == (row % (2 * N_DIFF_HEADS)) // 2, accn, 0.0)
        sr = lax.broadcasted_iota(jnp.int32, (BF16_ROWS, rd), 0)
        sc = lax.broadcasted_iota(jnp.int32, (BF16_ROWS, rd), 1)
        o_maps = []
        for mp in range(2):
            sel = ((sc // (2 * N_DIFF_HEADS) == sr) & (sc % 2 == mp)).astype(BF16)
            o_maps.append(_dot_sel_lhs(sel, accn))
        od = o_maps[0] - lam * o_maps[1]
        gi = lax.broadcasted_iota(jnp.int32, (width, width), 0) // HEAD_DIM
        gj = lax.broadcasted_iota(jnp.int32, (width, width), 1) // HEAD_DIM
        gmat = jnp.where(gi == gj, 1.0 / HEAD_DIM, 0.0).astype(BF16)
        ms = _dot_sel_rhs(od * od, gmat)
        od_ref[0] = (od * lax.rsqrt(ms + RMS_EPS) * gain_ref[...] * (1.0 - lam_init)).astype(od_ref.dtype)

        accfn = accf_ref[...] / lfs_ref[:, 0:1]
        col_hf = lax.broadcasted_iota(jnp.int32, (rf, width), 1) // HEAD_DIM
        row_f = lax.broadcasted_iota(jnp.int32, (rf, width), 0)
        accfn = jnp.where(col_hf == row_f % N_FOX_HEADS, accfn, 0.0)
        srf = lax.broadcasted_iota(jnp.int32, (BF16_ROWS, rf), 0)
        scf = lax.broadcasted_iota(jnp.int32, (BF16_ROWS, rf), 1)
        self_ = (scf // N_FOX_HEADS == srf).astype(BF16)
        of_ref[0] = _dot_sel_lhs(self_, accfn).astype(of_ref.dtype)


def _sample_attention(page_table, qd_bd, qf_bd, caches, selfs, lambdas, gain512, lam_init, dec_seq):
    nb, n_pages = page_table.shape
    page = caches[0].shape[2]
    assert page == LANES
    rd, rf = qd_bd.shape[1], qf_bd.shape[1]
    pp = next(c for c in (4, 2, 1) if n_pages % c == 0)
    n_steps = n_pages // pp

    def page_map(slot):
        return lambda b, j, pt: (pt[b, n_pages - 1 - (j * pp + slot)], 0, 0)

    per_b = lambda b, j, pt: (b, 0, 0)
    small = lambda a: pl.BlockSpec(a.shape, lambda b, j, pt: (0,) * a.ndim)
    b_spec = lambda a: pl.BlockSpec((1,) + a.shape[1:], per_b)
    out_spec = pl.BlockSpec((1, BF16_ROWS, DIFF_W), per_b)
    cache_specs, cache_args = [], []
    for a in caches:
        for slot in range(pp):
            cache_specs.append(pl.BlockSpec((1,) + a.shape[1:], page_map(slot)))
            cache_args.append(a)
    grid_spec = pltpu.PrefetchScalarGridSpec(
        num_scalar_prefetch=1,
        grid=(nb, n_steps),
        in_specs=[small(lambdas[0])] * 4 + [small(gain512), b_spec(qd_bd), b_spec(qf_bd)]
                 + cache_specs + [b_spec(a) for a in selfs],
        out_specs=[out_spec, out_spec],
        scratch_shapes=[pltpu.VMEM((rd, LANES), F32), pltpu.VMEM((rd, LANES), F32), pltpu.VMEM((rd, DIFF_W), F32),
                        pltpu.VMEM((rf, LANES), F32), pltpu.VMEM((rf, LANES), F32), pltpu.VMEM((rf, FOX_W), F32),
                        pltpu.VMEM((rf, LANES), F32), pltpu.VMEM((rf, LANES), F32)],
    )
    return pl.pallas_call(
        functools.partial(_sample_attn_kernel, dec_seq=dec_seq, lam_init=lam_init, pages_per_step=pp),
        grid_spec=grid_spec,
        out_shape=[jax.ShapeDtypeStruct((nb, BF16_ROWS, DIFF_W), BF16)] * 2,
        compiler_params=_cparams(("parallel", "arbitrary")),
        name="sample_attention",
    )(page_table, *lambdas, gain512, qd_bd, qf_bd, *cache_args, *selfs)


def _merge_kernel(x_ref, od_ref, of_ref, wo_ref, g_ref, b_ref, wrt_ref, rb_ref, h_ref, hb_ref, gt_ref, *, alpha):
    o = jnp.concatenate([od_ref[...], of_ref[...]], axis=1)
    mix = jnp.dot(o, wo_ref[...], preferred_element_type=F32)
    h = _layer_norm(alpha * x_ref[...] + mix, g_ref[...], b_ref[...])
    h_ref[...] = h
    hb_ref[...] = h.astype(BF16)

    tm = h.shape[0]
    h1, h2, h3 = _split3(h)
    w1, w2, w3 = _split3(wrt_ref[...])
    nt = lambda a, b: lax.dot_general(a, b, NT_DIMS, preferred_element_type=F32)
    logits = (nt(w1, h1) + (nt(w1, h2) + nt(w2, h1)) + (nt(w1, h3) + nt(w2, h2) + nt(w3, h1)))
    scores = jax.nn.sigmoid(logits)
    choice = scores + rb_ref[...]
    member = lax.broadcasted_iota(jnp.int32, (GROUP_SIZE, tm), 0)
    blocks, gscore = [], []
    for g in range(N_EXPERT_GROUPS):
        blk = choice[g * GROUP_SIZE:(g + 1) * GROUP_SIZE]
        m1 = jnp.max(blk, axis=0, keepdims=True)
        first = jnp.min(jnp.where(blk == m1, member, GROUP_SIZE), axis=0, keepdims=True)
        m2 = jnp.max(jnp.where(member == first, -jnp.inf, blk), axis=0, keepdims=True)
        blocks.append(blk)
        gscore.append(m1 + m2)
    masked = []
    for g in range(N_EXPERT_GROUPS):
        rank = jnp.zeros((1, tm), jnp.int32)
        for g2 in range(N_EXPERT_GROUPS):
            if g2 == g:
                continue
            beats = (gscore[g2] > gscore[g]) | ((gscore[g2] == gscore[g]) & (g2 < g))
            rank = rank + beats.astype(jnp.int32)
        masked.append(jnp.where(rank < TOPK_GROUPS, blocks[g], NEG_INF))
    vm = jnp.concatenate(masked, axis=0)
    eidx = lax.broadcasted_iota(jnp.int32, (N_EXPERTS, tm), 0)
    cnt = jnp.zeros((N_EXPERTS, tm), jnp.int32)
    for e2 in range(N_EXPERTS):
        r = vm[e2:e2 + 1]
        beats = (r > vm) | ((r == vm) & (e2 < eidx))
        cnt = cnt + beats.astype(jnp.int32)
    w = jnp.where(cnt < TOP_K, scores, 0.0)
    gt_ref[...] = w / (jnp.sum(w, axis=0, keepdims=True) + 1e-20) * ROUTED_SCALE


def _merge(x2d, od, of, wo_bf, g, b, wrt, rb, alpha, tm):
    n, d = x2d.shape
    row = lambda w: pl.BlockSpec((tm, w), lambda r: (r, 0))
    full = lambda a: pl.BlockSpec(a.shape, lambda r: (0,) * a.ndim)
    return pl.pallas_call(
        functools.partial(_merge_kernel, alpha=alpha),
        grid=(n // tm,),
        in_specs=[row(d), row(DIFF_W), row(FOX_W), full(wo_bf), full(g), full(b), full(wrt), full(rb)],
        out_specs=[row(d), row(d), pl.BlockSpec((N_EXPERTS, tm), lambda r: (0, r))],
        out_shape=[jax.ShapeDtypeStruct((n, d), F32), jax.ShapeDtypeStruct((n, d), BF16),
                   jax.ShapeDtypeStruct((N_EXPERTS, n), F32)],
        compiler_params=_cparams(("parallel",)),
        name="merge_ln_router",
    )(x2d, od, of, wo_bf, g, b, wrt, rb)


def _swiglu(hb, wgu, f):
    au = jnp.dot(hb, wgu, preferred_element_type=F32)
    return jax.nn.silu(au[:, :f]) * au[:, f:]


def _moe_kernel(hb_ref, h_ref, g_ref, wgu_ref, wd_ref, wsgu_ref, wsd_ref, ln_g_ref, ln_b_ref, y_ref, acc_ref, *, alpha):
    e = pl.program_id(1)
    hb = hb_ref[...]
    f = wd_ref.shape[1]
    gates = g_ref[...]
    lane = lax.broadcasted_iota(jnp.int32, gates.shape, 1)
    gcol = jnp.sum(jnp.where(lane == e, gates, 0.0), axis=1, keepdims=True)
    act = (_swiglu(hb, wgu_ref[0], f) * gcol).astype(BF16)
    contrib = jnp.dot(act, wd_ref[0], preferred_element_type=F32)

    @pl.when(e == 0)
    def _():
        fs = wsd_ref.shape[0]
        shared = jnp.dot(_swiglu(hb, wsgu_ref[...], fs).astype(BF16), wsd_ref[...], preferred_element_type=F32)
        acc_ref[...] = contrib + shared

    @pl.when(e > 0)
    def _():
        acc_ref[...] += contrib

    @pl.when(e == pl.num_programs(1) - 1)
    def _():
        y_ref[...] = _layer_norm(alpha * h_ref[...] + acc_ref[...], ln_g_ref[...], ln_b_ref[...])


def _moe(hb, h, gates, wgu, wd, wsgu, wsd, ln_g, ln_b, alpha, tn):
    n, d = h.shape
    ne = wgu.shape[0]
    row = lambda w: pl.BlockSpec((tn, w), lambda i, e: (i, 0))
    full = lambda a: pl.BlockSpec(a.shape, lambda i, e: (0,) * a.ndim)
    return pl.pallas_call(
        functools.partial(_moe_kernel, alpha=alpha),
        grid=(n // tn, ne),
        in_specs=[row(d), row(d), row(ne),
                  pl.BlockSpec((1,) + wgu.shape[1:], lambda i, e: (e, 0, 0)),
                  pl.BlockSpec((1,) + wd.shape[1:], lambda i, e: (e, 0, 0)),
                  full(wsgu), full(wsd), full(ln_g), full(ln_b)],
        out_specs=row(d),
        out_shape=jax.ShapeDtypeStruct((n, d), F32),
        scratch_shapes=[pltpu.VMEM((tn, d), F32)],
        compiler_params=_cparams(("parallel", "arbitrary")),
        name="moe_ln",
    )(hb, h, gates, wgu, wd, wsgu, wsd, ln_g, ln_b)


def _pick_tile(n, pref):
    t = min(n, pref)
    assert n % t == 0
    return t


def _feature_major(a):
    n, tokens = a.shape[:2]
    return jnp.moveaxis(a.reshape(n, tokens, -1), 1, 2)


def _token_major(a, feat_shape):
    n, _, tokens = a.shape
    return jnp.moveaxis(a, 1, 2).reshape((n, tokens) + feat_shape)


def kernel(x_prompt, x_sample, cache_diff_k, cache_diff_v, cache_fox_k, cache_fox_v, cache_fox_logf, page_table,
           w_in, b_forget, lambda_q1, lambda_k1, lambda_q2, lambda_k2, subln_gain, w_o, ln1_g, ln1_b, w_router,
           router_bias, w_exp_gate, w_exp_up, w_exp_down, w_sh_gate, w_sh_up, w_sh_down, ln2_g, ln2_b):
    depth = w_in.shape[0]
    nb, seq, d_model = x_prompt.shape
    dec_b, dec_seq, _ = x_sample.shape
    n_pages = page_table.shape[1]
    page = cache_diff_k.shape[2]
    past_len = n_pages * page
    alpha = (2.0 * depth) ** 0.25
    n_s = dec_b * dec_seq

    tq = _pick_tile(seq, 256)
    tm_p = _pick_tile(seq, 256)
    tabs_p = _rope_tables(seq, 0, seq)
    tabs_s = _rope_tables(n_s, past_len, dec_seq)

    xp = x_prompt.reshape(nb * seq, d_model)
    xs = x_sample.reshape(n_s, d_model)
    outs_p, outs_s = [], []
    for l in range(depth):
        lam_init = 0.8 - 0.6 * math.exp(-0.3 * l)
        w = w_in[l]
        cuts = [0, DIFF_W, 2 * DIFF_W, 3 * DIFF_W, 3 * DIFF_W + FOX_W, 3 * DIFF_W + 2 * FOX_W, 3 * DIFF_W + 3 * FOX_W]
        wqd, wkd, wvd, wqf, wkf, wvf = [w[:, cuts[i]:cuts[i + 1]] for i in range(6)]
        wfl = w[:, cuts[6]:]
        pw = {
            "wt": jnp.concatenate([wqd, wqf, wkd, wkf, wvd, wvf], axis=1).T.astype(BF16),
            "wk": jnp.concatenate([wkd, wkf], axis=1).astype(BF16),
            "wtfl": jnp.pad(wfl.T, ((0, BF16_ROWS - N_FOX_HEADS), (0, 0))).astype(BF16),
            "btfl": b_forget[l][:, None],
        }
        lambdas = [v[l][None, :] for v in (lambda_q1, lambda_k1, lambda_q2, lambda_k2)]
        gain = subln_gain[l]
        gain128 = jnp.tile(gain, LANES // HEAD_DIM)[None, :]
        gain512 = jnp.tile(gain, DIFF_W // HEAD_DIM)[None, :]
        wo_bf = w_o[l].astype(BF16)
        wrt = w_router[l].T
        rb = router_bias[l][:, None]
        wgu = jnp.concatenate([w_exp_gate[l], w_exp_up[l]], axis=2).astype(BF16)
        wd = w_exp_down[l].astype(BF16)
        wsgu = jnp.concatenate([w_sh_gate[l], w_sh_up[l]], axis=1).astype(BF16)
        wsd = w_sh_down[l].astype(BF16)
        g1, b1 = ln1_g[l][None, :], ln1_b[l][None, :]
        g2, b2 = ln2_g[l][None, :], ln2_b[l][None, :]

        def ffn(x2d, od, of, tm, tn):
            h, hb, gt = _merge(x2d, od, of, wo_bf, g1, b1, wrt, rb, alpha, tm)
            return _moe(hb, h, gt.T, wgu, wd, wsgu, wsd, g2, b2, alpha, tn)

        (qdt, qft, kdt, kft, vdt, vft, kdb, kfb, v1d, v1f, logft) = _project(xp, nb, seq, pw, tabs_p, tm_p, tq)
        ct, terms = _cumsum_t(logft)
        n_pairs = N_FOX_HEADS // 2
        cqt_pairs = ct.reshape(nb, n_pairs, 2, seq)
        kbias_pairs = jnp.transpose(terms.reshape(nb, 3, n_pairs, 2, seq), (0, 2, 4, 3, 1)).reshape(nb, n_pairs, seq, 6)
        kbias_pairs = jnp.pad(kbias_pairs, ((0, 0), (0, 0), (0, 0), (0, LANES - 6))).astype(BF16)
        od, of = _prompt_attention(qdt, kdb.reshape(nb, seq, DIFF_W), v1d, qft, kfb.reshape(nb, seq, FOX_W), v1f,
                                   kbias_pairs, cqt_pairs, lambdas, gain128, lam_init, tq)
        xp_new = ffn(xp, od.reshape(nb * seq, DIFF_W), of.reshape(nb * seq, FOX_W),
                     _pick_tile(nb * seq, 256), _pick_tile(nb * seq, 1024))
        outs_p.append((kdt, vdt, kft, vft, logft))

        (sqdt, sqft, skdt, skft, svdt, svft, _, _, _, _, slogft) = _project(xs, 1, n_s, pw, tabs_s, n_s, n_s)
        qd_rows = sqdt[0].T.reshape(dec_b, dec_seq, 2 * N_DIFF_HEADS, DIFF_D)
        eye_hm = jnp.eye(2 * N_DIFF_HEADS, dtype=BF16)
        qd_bd = (qd_rows[:, :, :, None, :] * eye_hm[None, None, :, :, None]).reshape(
            dec_b, dec_seq * 2 * N_DIFF_HEADS, DIFF_W)
        qf_rows = sqft[0].T.reshape(dec_b, dec_seq, N_FOX_HEADS, HEAD_DIM)
        eye_h = jnp.eye(N_FOX_HEADS, dtype=BF16)
        qf_bd = (qf_rows[:, :, :, None, :] * eye_h[None, None, :, :, None]).reshape(
            dec_b, dec_seq * N_FOX_HEADS, FOX_W)
        caches = [_feature_major(c[l]) for c in (cache_diff_k, cache_diff_v, cache_fox_k, cache_fox_v, cache_fox_logf)]
        self_page = lambda a: jnp.pad(jnp.transpose(a[0].reshape(-1, dec_b, dec_seq), (1, 0, 2)),
                                      ((0, 0), (0, 0), (0, page - dec_seq)))
        slf_t = jnp.tile(self_page(slogft), (1, dec_seq, 1))
        selfs = [self_page(skdt), self_page(svdt), self_page(skft), self_page(svft), slf_t]
        od_s, of_s = _sample_attention(page_table, qd_bd, qf_bd, caches, selfs, lambdas, gain512, lam_init, dec_seq)
        od_s = od_s[:, :dec_seq].reshape(n_s, DIFF_W)
        of_s = of_s[:, :dec_seq].reshape(n_s, FOX_W)
        xs_new = ffn(xs, od_s, of_s, n_s, n_s)
        outs_s.append(tuple(a.reshape(a.shape[1], dec_b, dec_seq) for a in (skdt, svdt, skft, svft, slogft)))
        xp, xs = xp_new, xs_new

    def stack_p(idx, feat_shape):
        return jnp.stack([_token_major(o[idx], feat_shape) for o in outs_p])

    def stack_s(idx, feat_shape):
        return jnp.stack([jnp.transpose(o[idx], (1, 2, 0)).reshape((dec_b, dec_seq) + feat_shape) for o in outs_s])

    shapes = ((N_DIFF_HEADS, 2, DIFF_D), (N_DIFF_HEADS, HEAD_DIM), (N_FOX_HEADS, HEAD_DIM), (N_FOX_HEADS, HEAD_DIM),
              (N_FOX_HEADS,))
    return (xp.reshape(nb, seq, d_model), xs.reshape(dec_b, dec_seq, d_model),
            *[stack_p(i, s) for i, s in enumerate(shapes)], *[stack_s(i, s) for i, s in enumerate(shapes)])
```

```python
import functools
import math

import jax
import jax.numpy as jnp
from jax import lax
from jax.experimental import pallas as pl
from jax.experimental.pallas import tpu as pltpu

F32 = jnp.float32
BF16 = jnp.bfloat16

HEAD_DIM = 64
N_DIFF_HEADS = 8
N_FOX_HEADS = 8
DIFF_D = HEAD_DIM // 2
DIFF_W = N_DIFF_HEADS * HEAD_DIM
FOX_W = N_FOX_HEADS * HEAD_DIM
ROT_DIM = DIFF_D // 4
ROT_HALF = ROT_DIM // 2
ROPE_THETA = 500000.0
N_EXPERTS = 64
N_EXPERT_GROUPS = 8
GROUP_SIZE = N_EXPERTS // N_EXPERT_GROUPS
TOPK_GROUPS = 4
TOP_K = 8
ROUTED_SCALE = 2.5
LN_EPS = 1e-5
RMS_EPS = 1e-5
NEG_INF = -1e30
LOG2E = math.log2(math.e)

LANES = 128
SUBLANES = 8
BF16_ROWS = 16
V_ROWS = HEAD_DIM + BF16_ROWS
VMEM_LIMIT = 56 * 1024 * 1024
MOE_SUB_ROWS = 512
N_SCORE_BUFS = 4

NT_DIMS = (((1,), (1,)), ((), ()))


def _cparams(sem):
    return pltpu.CompilerParams(dimension_semantics=sem, vmem_limit_bytes=VMEM_LIMIT)


def _log_sigmoid(x):
    return jnp.minimum(x, 0.0) - jnp.log1p(jnp.exp(-jnp.abs(x)))


def _layer_norm(y, g, b):
    mu = jnp.mean(y, axis=-1, keepdims=True)
    d = y - mu
    var = jnp.mean(d * d, axis=-1, keepdims=True)
    return d * lax.rsqrt(var + LN_EPS) * g + b


def _split3(a):
    a1 = a.astype(BF16)
    r1 = a - a1.astype(F32)
    a2 = r1.astype(BF16)
    a3 = (r1 - a2.astype(F32)).astype(BF16)
    return a1, a2, a3


def _dot_sel_lhs(sel, b):
    return sum(jnp.dot(sel, t, preferred_element_type=F32) for t in _split3(b))


def _dot_sel_rhs(a, sel):
    return sum(jnp.dot(t, sel, preferred_element_type=F32) for t in _split3(a))


def _lambda_full(lq1_ref, lk1_ref, lq2_ref, lk2_ref, lam_init):
    a = jnp.sum(lq1_ref[...] * lk1_ref[...], axis=1, keepdims=True)
    b = jnp.sum(lq2_ref[...] * lk2_ref[...], axis=1, keepdims=True)
    return jnp.exp(a) - jnp.exp(b) + lam_init


def _rope_table_kernel(invf_lane_ref, invf_sub_ref, c_ref, s1_ref, s2_ref, ct_ref, st_ref, *, base, mod, tm):
    i = pl.program_id(0)
    row = lax.broadcasted_iota(jnp.int32, (tm, LANES), 0) + i * tm
    pos = (base + row % mod).astype(F32)
    ang = pos * invf_lane_ref[...]
    li = lax.broadcasted_iota(jnp.int32, (tm, LANES), 1) % DIFF_D
    c = jnp.cos(ang)
    s = jnp.sin(ang)
    c_ref[...] = jnp.where(li < ROT_DIM, c, 1.0)
    s1_ref[...] = jnp.where(li < ROT_HALF, -s, 0.0)
    s2_ref[...] = jnp.where((li >= ROT_HALF) & (li < ROT_DIM), s, 0.0)
    col = lax.broadcasted_iota(jnp.int32, (SUBLANES, tm), 1) + i * tm
    pos_t = (base + col % mod).astype(F32)
    ang_t = invf_sub_ref[...] * pos_t
    sub = lax.broadcasted_iota(jnp.int32, (SUBLANES, tm), 0)
    ct_ref[...] = jnp.cos(ang_t)
    st_ref[...] = jnp.where(sub < ROT_HALF, -jnp.sin(ang_t), jnp.sin(ang_t))


def _rope_tables(n_pos, base, mod):
    inv4 = jnp.power(ROPE_THETA, -jnp.arange(ROT_HALF, dtype=F32) * 2.0 / ROT_DIM)
    invf_lane = inv4[(jnp.arange(LANES) % DIFF_D) % ROT_HALF][None, :]
    invf_sub = inv4[jnp.arange(SUBLANES) % ROT_HALF][:, None]
    tm = min(n_pos, 1024)
    assert n_pos % tm == 0
    lane_spec = pl.BlockSpec((tm, LANES), lambda i: (i, 0))
    sub_spec = pl.BlockSpec((SUBLANES, tm), lambda i: (0, i))
    return pl.pallas_call(
        functools.partial(_rope_table_kernel, base=base, mod=mod, tm=tm),
        grid=(n_pos // tm,),
        in_specs=[pl.BlockSpec((1, LANES), lambda i: (0, 0)), pl.BlockSpec((SUBLANES, 1), lambda i: (0, 0))],
        out_specs=[lane_spec, lane_spec, lane_spec, sub_spec, sub_spec],
        out_shape=[jax.ShapeDtypeStruct((n_pos, LANES), F32)] * 3 + [jax.ShapeDtypeStruct((SUBLANES, n_pos), F32)] * 2,
        compiler_params=_cparams(("parallel",)),
        name="rope_tables",
    )(invf_lane, invf_sub)


def _rope_sublanes(zt, ct, st):
    rows = []
    for j in range(DIFF_W // DIFF_D):
        x8 = zt[DIFF_D * j:DIFF_D * j + ROT_DIM]
        rows.append(x8 * ct + pltpu.roll(x8, ROT_HALF, 0) * st)
        rows.append(zt[DIFF_D * j + ROT_DIM:DIFF_D * (j + 1)])
    return jnp.concatenate(rows, axis=0)


def _proj_kernel(x_ref, wt_ref, wk_ref, wtfl_ref, btfl_ref, c_ref, s1_ref, s2_ref, ct_ref, st_ref,
                 qdt_ref, qft_ref, kdt_ref, kft_ref, vdt_ref, vft_ref, kdb_ref, kfb_ref, v1d_ref, v1f_ref, logft_ref,
                 *, tm, tk):
    xb = x_ref[...].astype(BF16)
    zt = lax.dot_general(wt_ref[...], xb, NT_DIMS, preferred_element_type=F32)
    ct, st = ct_ref[...], st_ref[...]
    w = DIFF_W
    qdt_ref[0] = (_rope_sublanes(zt[0:w], ct, st) * (DIFF_D ** -0.5 * LOG2E)).astype(BF16)
    qft_ref[0] = (zt[w:2 * w] * (HEAD_DIM ** -0.5 * LOG2E)).astype(BF16)
    kdt_ref[0] = _rope_sublanes(zt[2 * w:3 * w], ct, st)
    kft_ref[0] = zt[3 * w:4 * w]
    vd = zt[4 * w:5 * w]
    vf = zt[5 * w:6 * w]
    vdt_ref[0] = vd
    vft_ref[0] = vf
    ones = jnp.ones((BF16_ROWS, tk), BF16)
    for h in range(N_DIFF_HEADS):
        for t in range(tm // tk):
            v1d_ref[0, h, t, 0:HEAD_DIM, :] = vd[HEAD_DIM * h:HEAD_DIM * (h + 1), t * tk:(t + 1) * tk].astype(BF16)
            v1d_ref[0, h, t, HEAD_DIM:V_ROWS, :] = ones
            v1f_ref[0, h, t, 0:HEAD_DIM, :] = vf[HEAD_DIM * h:HEAD_DIM * (h + 1), t * tk:(t + 1) * tk].astype(BF16)
            v1f_ref[0, h, t, HEAD_DIM:V_ROWS, :] = ones
    flt = lax.dot_general(wtfl_ref[...], xb, NT_DIMS, preferred_element_type=F32)
    logft_ref[0] = _log_sigmoid(flt[:N_FOX_HEADS] + btfl_ref[...])

    zk = jnp.dot(xb, wk_ref[...], preferred_element_type=F32)
    c, s1, s2 = c_ref[...], s1_ref[...], s2_ref[...]
    pieces = []
    for j in range(DIFF_W // LANES):
        xk = zk[:, j * LANES:(j + 1) * LANES]
        pieces.append(xk * c + pltpu.roll(xk, LANES - ROT_HALF, 1) * s1 + pltpu.roll(xk, ROT_HALF, 1) * s2)
    kdb_ref[...] = jnp.concatenate(pieces, axis=1).astype(BF16)
    kfb_ref[...] = zk[:, DIFF_W:].astype(BF16)


def _project(x2d, n_batch, seq, w, tables, tm, tk):
    n, d_model = x2d.shape
    nt = seq // tm
    c, s1, s2, ct, st = tables
    full = lambda a: pl.BlockSpec(a.shape, lambda r: (0,) * a.ndim)
    row512 = pl.BlockSpec((tm, DIFF_W), lambda r: (r, 0))
    tab = pl.BlockSpec((tm, LANES), lambda r: (r % nt, 0))
    tabt = pl.BlockSpec((SUBLANES, tm), lambda r: (0, r % nt))
    ft_spec = pl.BlockSpec((1, DIFF_W, tm), lambda r: (r // nt, 0, r % nt))
    v1_spec = pl.BlockSpec((1, N_DIFF_HEADS, tm // tk, V_ROWS, tk), lambda r: (r // nt, 0, r % nt, 0, 0))
    lt_spec = pl.BlockSpec((1, N_FOX_HEADS, tm), lambda r: (r // nt, 0, r % nt))
    sds = jax.ShapeDtypeStruct
    return pl.pallas_call(
        functools.partial(_proj_kernel, tm=tm, tk=tk),
        grid=(n // tm,),
        in_specs=[pl.BlockSpec((tm, d_model), lambda r: (r, 0)),
                  full(w["wt"]), full(w["wk"]), full(w["wtfl"]), full(w["btfl"]), tab, tab, tab, tabt, tabt],
        out_specs=[ft_spec] * 6 + [row512, row512, v1_spec, v1_spec, lt_spec],
        out_shape=[sds((n_batch, DIFF_W, seq), BF16)] * 2 + [sds((n_batch, DIFF_W, seq), F32)] * 4
                  + [sds((n, DIFF_W), BF16)] * 2
                  + [sds((n_batch, N_DIFF_HEADS, seq // tk, V_ROWS, tk), BF16)] * 2
                  + [sds((n_batch, N_FOX_HEADS, seq), F32)],
        compiler_params=_cparams(("parallel",)),
        name="in_proj",
    )(x2d, w["wt"], w["wk"], w["wtfl"], w["btfl"], c, s1, s2, ct, st)


def _lane_cumsum(x):
    lane = lax.broadcasted_iota(jnp.int32, x.shape, 1)
    s = 1
    while s < LANES:
        x = x + jnp.where(lane >= s, pltpu.roll(x, s, 1), 0.0)
        s *= 2
    return x


def _lane_rev_cumsum(x):
    lane = lax.broadcasted_iota(jnp.int32, x.shape, 1)
    s = 1
    while s < LANES:
        x = x + jnp.where(lane < LANES - s, pltpu.roll(x, LANES - s, 1), 0.0)
        s *= 2
    return x


def _cumsum_kernel(lft_ref, ct_ref, terms_ref, *, seq):
    carry = jnp.zeros((N_FOX_HEADS, 1), F32)
    for t in range(seq // LANES):
        x = _lane_cumsum(lft_ref[0, :, t * LANES:(t + 1) * LANES]) + carry
        ct_ref[0, :, t * LANES:(t + 1) * LANES] = x
        for n, term in enumerate(_split3(-LOG2E * x)):
            terms_ref[0, n, :, t * LANES:(t + 1) * LANES] = term.astype(F32)
        carry = x[:, LANES - 1:LANES]


def _cumsum_t(logft):
    nb, nh, seq = logft.shape
    spec = pl.BlockSpec((1, nh, seq), lambda b: (b, 0, 0))
    return pl.pallas_call(
        functools.partial(_cumsum_kernel, seq=seq),
        grid=(nb,), in_specs=[spec],
        out_specs=[spec, pl.BlockSpec((1, 3, nh, seq), lambda b: (b, 0, 0, 0))],
        out_shape=[jax.ShapeDtypeStruct(logft.shape, F32), jax.ShapeDtypeStruct((nb, 3, nh, seq), F32)],
        compiler_params=_cparams(("parallel",)),
        name="logf_cumsum",
    )(logft)


def _block_diag_q(qt, nblk):
    rows_per = LANES // nblk
    rb = lax.broadcasted_iota(jnp.int32, qt.shape, 0) // rows_per
    zero = jnp.zeros_like(qt)
    return jnp.concatenate([jnp.where(rb == j, qt, zero) for j in range(nblk)], axis=1)


def _scores(j, k_ref, qbd_ref, s_ref, tk, kbias_ref=None):
    start = pl.multiple_of(j * tk, tk)
    k = k_ref[0, pl.ds(start, tk), :]
    if kbias_ref is not None:
        k = jnp.concatenate([k, kbias_ref[0, 0, pl.ds(start, tk), :]], axis=1)
    s_ref[...] = jnp.dot(k, qbd_ref[...], preferred_element_type=F32)


def _softmax_pv(j, s_ref, vt_ref, acc_ref, m_ref, *, tq, nblk, blk_head, masked, cq=None):
    tk = tq
    if masked:
        keep = (lax.broadcasted_iota(jnp.int32, (tk, tq), 0) <= lax.broadcasted_iota(jnp.int32, (tk, tq), 1))
    for blk in range(nblk):
        sb = s_ref[:, blk * tq:(blk + 1) * tq]
        if masked:
            sb = jnp.where(keep, sb, NEG_INF)
        m_old = m_ref[blk, 0:1, :]
        smax = jnp.max(sb, axis=0, keepdims=True)
        if cq is not None:
            m_new = jnp.maximum(m_old, smax + cq[blk])
            shift = m_new - cq[blk]
        else:
            m_new = jnp.maximum(m_old, smax)
            shift = m_new
        p = jnp.exp2(sb - shift).astype(BF16)
        alpha = jnp.exp2(m_old - m_new)
        vt = vt_ref[0, blk_head[blk], j]
        acc_ref[blk] = acc_ref[blk] * alpha + jnp.dot(vt, p, preferred_element_type=F32)
        m_ref[blk] = jnp.broadcast_to(m_new, (SUBLANES, tq))


def _causal_sweep(i, scores, update):
    scores(0, 0)

    def body(jj, carry):
        j = N_SCORE_BUFS * jj
        for r in range(N_SCORE_BUFS):
            scores(j + r + 1, (r + 1) % N_SCORE_BUFS)
            update(j + r, r, False)
        return carry

    lax.fori_loop(0, i // N_SCORE_BUFS, body, 0)
    rem = i % N_SCORE_BUFS
    j0 = i - rem
    for r in range(N_SCORE_BUFS - 1):
        @pl.when(r < rem)
        def _():
            scores(j0 + r + 1, r + 1)
            update(j0 + r, r, False)

    for r in range(N_SCORE_BUFS):
        @pl.when(rem == r)
        def _():
            update(i, r, True)


def _attn_init(acc_ref, m_ref):
    acc_ref[...] = jnp.zeros(acc_ref.shape, F32)
    m_ref[...] = jnp.full(m_ref.shape, NEG_INF, F32)


def _diff_attn_kernel(lq1_ref, lk1_ref, lq2_ref, lk2_ref, gain_ref, qt_ref, k_ref, vt_ref, o_ref,
                      acc_ref, m_ref, qbd_ref, *s_refs, tq, lam_init):
    i = pl.program_id(2)
    nblk = 4
    qbd_ref[...] = _block_diag_q(qt_ref[0], nblk)
    _attn_init(acc_ref, m_ref)
    scores = lambda j, slot: _scores(j, k_ref, qbd_ref, s_refs[slot], tq)
    update = lambda j, slot, masked: _softmax_pv(j, s_refs[slot], vt_ref, acc_ref, m_ref, tq=tq, nblk=nblk,
                                                 blk_head=(0, 0, 1, 1), masked=masked)
    _causal_sweep(i, scores, update)

    lam = _lambda_full(lq1_ref, lk1_ref, lq2_ref, lk2_ref, lam_init)
    outs = []
    for hh in range(2):
        a1 = acc_ref[2 * hh]
        a2 = acc_ref[2 * hh + 1]
        o = a1[0:HEAD_DIM] / a1[HEAD_DIM:HEAD_DIM + 1] - lam * (a2[0:HEAD_DIM] / a2[HEAD_DIM:HEAD_DIM + 1])
        ms = jnp.mean(o * o, axis=0, keepdims=True)
        outs.append(o * lax.rsqrt(ms + RMS_EPS))
    o = jnp.concatenate(outs, axis=0).T
    o_ref[0] = (o * gain_ref[...] * (1.0 - lam_init)).astype(o_ref.dtype)


def _fox_attn_kernel(qt_ref, k_ref, vt_ref, kbias_ref, cqt_ref, o_ref, acc_ref, m_ref, qbd_ref, *s_refs, tq):
    i = pl.program_id(2)
    nblk = 2
    qbd_ref[0:LANES, :] = _block_diag_q(qt_ref[0], nblk)
    r = lax.broadcasted_iota(jnp.int32, (LANES, nblk * tq), 0)
    cb = lax.broadcasted_iota(jnp.int32, (LANES, nblk * tq), 1) // tq
    qbd_ref[LANES:2 * LANES, :] = jnp.where((r < 3 * nblk) & (r // 3 == cb), 1.0, 0.0).astype(BF16)
    _attn_init(acc_ref, m_ref)
    cq = [cqt_ref[0, 0, hh:hh + 1, :] * LOG2E for hh in range(nblk)]
    scores = lambda j, slot: _scores(j, k_ref, qbd_ref, s_refs[slot], tq, kbias_ref)
    update = lambda j, slot, masked: _softmax_pv(j, s_refs[slot], vt_ref, acc_ref, m_ref, tq=tq, nblk=nblk,
                                                 blk_head=(0, 1), masked=masked, cq=cq)
    _causal_sweep(i, scores, update)
    outs = []
    for hh in range(nblk):
        a = acc_ref[hh]
        outs.append(a[0:HEAD_DIM] / a[HEAD_DIM:HEAD_DIM + 1])
    o_ref[0] = jnp.concatenate(outs, axis=0).T.astype(o_ref.dtype)


def _prompt_attention(qdt, kdb, v1d, qft, kfb, v1f, kbias_pairs, cqt_pairs, lambdas, gain128, lam_init, tq):
    nb, _, seq = qdt.shape
    n_pairs = DIFF_W // LANES
    grid = (nb, n_pairs, seq // tq)
    qt_spec = pl.BlockSpec((1, LANES, tq), lambda b, p, i: (b, p, i))
    k_spec = pl.BlockSpec((1, seq, LANES), lambda b, p, i: (b, 0, p))
    vt_spec = pl.BlockSpec((1, 2, seq // tq, V_ROWS, tq), lambda b, p, i: (b, p, 0, 0, 0))
    o_spec = pl.BlockSpec((1, tq, LANES), lambda b, p, i: (b, i, p))
    small = lambda a: pl.BlockSpec(a.shape, lambda b, p, i: (0,) * a.ndim)
    sem = ("parallel", "parallel", "arbitrary")

    def scratch(nblk, k_width):
        return [pltpu.VMEM((nblk, V_ROWS, tq), F32), pltpu.VMEM((nblk, SUBLANES, tq), F32),
                pltpu.VMEM((k_width, nblk * tq), BF16)] + [pltpu.VMEM((tq, nblk * tq), F32)] * N_SCORE_BUFS

    od = pl.pallas_call(
        functools.partial(_diff_attn_kernel, tq=tq, lam_init=lam_init),
        grid=grid,
        in_specs=[small(lambdas[0])] * 4 + [small(gain128), qt_spec, k_spec, vt_spec],
        out_specs=o_spec,
        out_shape=jax.ShapeDtypeStruct((nb, seq, DIFF_W), BF16),
        scratch_shapes=scratch(4, LANES),
        compiler_params=_cparams(sem),
        name="diff_attention",
    )(*lambdas, gain128, qdt, kdb, v1d)
    kbias_spec = pl.BlockSpec((1, 1, seq, LANES), lambda b, p, i: (b, p, 0, 0))
    cq_spec = pl.BlockSpec((1, 1, 2, tq), lambda b, p, i: (b, p, 0, i))
    of = pl.pallas_call(
        functools.partial(_fox_attn_kernel, tq=tq),
        grid=grid,
        in_specs=[qt_spec, k_spec, vt_spec, kbias_spec, cq_spec],
        out_specs=o_spec,
        out_shape=jax.ShapeDtypeStruct((nb, seq, FOX_W), BF16),
        scratch_shapes=scratch(2, 2 * LANES),
        compiler_params=_cparams(sem),
        name="fox_attention",
    )(qft, kfb, v1f, kbias_pairs, cqt_pairs)
    return od, of


def _row_softmax_update(s, vts, m_ref, l_ref, acc_ref):
    m_old = m_ref[:, 0:1]
    m_new = jnp.maximum(m_old, jnp.max(s, axis=1, keepdims=True))
    p = jnp.exp2(s - m_new)
    alpha = jnp.exp2(m_old - m_new)
    l_new = alpha * l_ref[:, 0:1] + jnp.sum(p, axis=1, keepdims=True)
    pb = p.astype(BF16)
    pv = sum(lax.dot_general(pb[:, i * LANES:(i + 1) * LANES], vt.astype(BF16), NT_DIMS, preferred_element_type=F32)
             for i, vt in enumerate(vts))
    acc_ref[...] = alpha * acc_ref[...] + pv
    m_ref[...] = jnp.broadcast_to(m_new, m_ref.shape)
    l_ref[...] = jnp.broadcast_to(l_new, l_ref.shape)


def _sample_attn_kernel(pt_ref, lq1_ref, lk1_ref, lq2_ref, lk2_ref, gain_ref, qd_ref, qf_ref, *rest,
                        dec_seq, lam_init, pages_per_step):
    pp = pages_per_step
    dk_refs, dv_refs, fk_refs, fv_refs, lf_refs = (rest[i * pp:(i + 1) * pp] for i in range(5))
    (sdk_ref, sdv_ref, sfk_ref, sfv_ref, slf_ref, od_ref, of_ref,
     md_ref, ld_ref, accd_ref, mf_ref, lfs_ref, accf_ref, carry_ref, cself_ref) = rest[5 * pp:]
    j = pl.program_id(1)
    rd = dec_seq * 2 * N_DIFF_HEADS
    rf = dec_seq * N_FOX_HEADS
    lane_d = lax.broadcasted_iota(jnp.int32, (rd, LANES), 1)
    t_d = lax.broadcasted_iota(jnp.int32, (rd, LANES), 0) // (2 * N_DIFF_HEADS)
    lane_f = lax.broadcasted_iota(jnp.int32, (rf, LANES), 1)
    t_f = lax.broadcasted_iota(jnp.int32, (rf, LANES), 0) // N_FOX_HEADS

    def scores(q_ref, kts):
        q = q_ref[0]
        return jnp.concatenate([jnp.dot(q, kt.astype(BF16), preferred_element_type=F32) for kt in kts], axis=1)

    @pl.when(j == 0)
    def _():
        md_ref[...] = jnp.full(md_ref.shape, NEG_INF, F32)
        ld_ref[...] = jnp.zeros(ld_ref.shape, F32)
        accd_ref[...] = jnp.zeros(accd_ref.shape, F32)
        mf_ref[...] = jnp.full(mf_ref.shape, NEG_INF, F32)
        lfs_ref[...] = jnp.zeros(lfs_ref.shape, F32)
        accf_ref[...] = jnp.zeros(accf_ref.shape, F32)
        carry_ref[...] = jnp.zeros(carry_ref.shape, F32)
        sd = jnp.where((lane_d < dec_seq) & (lane_d <= t_d), scores(qd_ref, [sdk_ref[0]]), NEG_INF)
        _row_softmax_update(sd, [sdv_ref[0]], md_ref, ld_ref, accd_ref)
        incl = _lane_cumsum(slf_ref[0])
        cs_t = jnp.sum(jnp.where(lane_f == t_f, incl, 0.0), axis=1, keepdims=True)
        cself_ref[...] = jnp.broadcast_to(cs_t, cself_ref.shape)
        sf = scores(qf_ref, [sfk_ref[0]]) + (cs_t - incl) * LOG2E
        sf = jnp.where((lane_f < dec_seq) & (lane_f <= t_f), sf, NEG_INF)
        _row_softmax_update(sf, [sfv_ref[0]], mf_ref, lfs_ref, accf_ref)

    _row_softmax_update(scores(qd_ref, [r[0] for r in dk_refs]), [r[0] for r in dv_refs], md_ref, ld_ref, accd_ref)
    cself = cself_ref[:, 0:1]
    carry = carry_ref[:, 0:1]
    biases = []
    for r in lf_refs:
        lf = jnp.concatenate([r[0]] * dec_seq, axis=0)
        suffix = _lane_rev_cumsum(lf)
        biases.append(cself + carry + (suffix - lf))
        carry = carry + suffix[:, 0:1]
    carry_ref[...] = jnp.broadcast_to(carry, carry_ref.shape)
    sf = scores(qf_ref, [r[0] for r in fk_refs]) + jnp.concatenate(biases, axis=1) * LOG2E
    _row_softmax_update(sf, [r[0] for r in fv_refs], mf_ref, lfs_ref, accf_ref)

    @pl.when(j == pl.num_programs(1) - 1)
    def _():
        lam = _lambda_full(lq1_ref, lk1_ref, lq2_ref, lk2_ref, lam_init)
        width = accd_ref.shape[1]
        accn = accd_ref[...] / ld_ref[:, 0:1]
        col_h = lax.broadcasted_iota(jnp.int32, (rd, width), 1) // HEAD_DIM
        row = lax.broadcasted_iota(jnp.int32, (rd, width), 0)
        accn = jnp.where(col_h == (row % (2 * N_DIFF_HEADS)) // 2, accn, 0.0)
        sr = lax.broadcasted_iota(jnp.int32, (BF16_ROWS, rd), 0)
        sc = lax.broadcasted_iota(jnp.int32, (BF16_ROWS, rd), 1)
        o_maps = []
        for mp in range(2):
            sel = ((sc // (2 * N_DIFF_HEADS) == sr) & (sc % 2 == mp)).astype(BF16)
            o_maps.append(_dot_sel_lhs(sel, accn))
        od = o_maps[0] - lam * o_maps[1]
        gi = lax.broadcasted_iota(jnp.int32, (width, width), 0) // HEAD_DIM
        gj = lax.broadcasted_iota(jnp.int32, (width, width), 1) // HEAD_DIM
        gmat = jnp.where(gi == gj, 1.0 / HEAD_DIM, 0.0).astype(BF16)
        ms = _dot_sel_rhs(od * od, gmat)
        od_ref[0] = (od * lax.rsqrt(ms + RMS_EPS) * gain_ref[...] * (1.0 - lam_init)).astype(od_ref.dtype)

        accfn = accf_ref[...] / lfs_ref[:, 0:1]
        col_hf = lax.broadcasted_iota(jnp.int32, (rf, width), 1) // HEAD_DIM
        row_f = lax.broadcasted_iota(jnp.int32, (rf, width), 0)
        accfn = jnp.where(col_hf == row_f % N_FOX_HEADS, accfn, 0.0)
        srf = lax.broadcasted_iota(jnp.int32, (BF16_ROWS, rf), 0)
        scf = lax.broadcasted_iota(jnp.int32, (BF16_ROWS, rf), 1)
        self_ = (scf // N_FOX_HEADS == srf).astype(BF16)
        of_ref[0] = _dot_sel_lhs(self_, accfn).astype(of_ref.dtype)


def _sample_attention(page_table, qd_bd, qf_bd, caches, selfs, lambdas, gain512, lam_init, dec_seq):
    nb, n_pages = page_table.shape
    page = caches[0].shape[2]
    assert page == LANES
    rd, rf = qd_bd.shape[1], qf_bd.shape[1]
    pp = next(c for c in (8, 4, 2, 1) if n_pages % c == 0)
    n_steps = n_pages // pp

    def page_map(slot):
        return lambda b, j, pt: (pt[b, n_pages - 1 - (j * pp + slot)], 0, 0)

    per_b = lambda b, j, pt: (b, 0, 0)
    small = lambda a: pl.BlockSpec(a.shape, lambda b, j, pt: (0,) * a.ndim)
    b_spec = lambda a: pl.BlockSpec((1,) + a.shape[1:], per_b)
    out_spec = pl.BlockSpec((1, BF16_ROWS, DIFF_W), per_b)
    cache_specs, cache_args = [], []
    for a in caches:
        for slot in range(pp):
            cache_specs.append(pl.BlockSpec((1,) + a.shape[1:], page_map(slot)))
            cache_args.append(a)
    grid_spec = pltpu.PrefetchScalarGridSpec(
        num_scalar_prefetch=1,
        grid=(nb, n_steps),
        in_specs=[small(lambdas[0])] * 4 + [small(gain512), b_spec(qd_bd), b_spec(qf_bd)]
                 + cache_specs + [b_spec(a) for a in selfs],
        out_specs=[out_spec, out_spec],
        scratch_shapes=[pltpu.VMEM((rd, LANES), F32), pltpu.VMEM((rd, LANES), F32), pltpu.VMEM((rd, DIFF_W), F32),
                        pltpu.VMEM((rf, LANES), F32), pltpu.VMEM((rf, LANES), F32), pltpu.VMEM((rf, FOX_W), F32),
                        pltpu.VMEM((rf, LANES), F32), pltpu.VMEM((rf, LANES), F32)],
    )
    return pl.pallas_call(
        functools.partial(_sample_attn_kernel, dec_seq=dec_seq, lam_init=lam_init, pages_per_step=pp),
        grid_spec=grid_spec,
        out_shape=[jax.ShapeDtypeStruct((nb, BF16_ROWS, DIFF_W), BF16)] * 2,
        compiler_params=_cparams(("parallel", "arbitrary")),
        name="sample_attention",
    )(page_table, *lambdas, gain512, qd_bd, qf_bd, *cache_args, *selfs)


def _merge_kernel(x_ref, od_ref, of_ref, wo_ref, g_ref, b_ref, wrt_ref, rb_ref, h_ref, hb_ref, gt_ref, *, alpha):
    o = jnp.concatenate([od_ref[...], of_ref[...]], axis=1)
    mix = jnp.dot(o, wo_ref[...], preferred_element_type=F32)
    h = _layer_norm(alpha * x_ref[...] + mix, g_ref[...], b_ref[...])
    h_ref[...] = h
    hb_ref[...] = h.astype(BF16)

    tm = h.shape[0]
    h1, h2, h3 = _split3(h)
    w1, w2, w3 = _split3(wrt_ref[...])
    nt = lambda a, b: lax.dot_general(a, b, NT_DIMS, preferred_element_type=F32)
    logits = (nt(w1, h1) + (nt(w1, h2) + nt(w2, h1)) + (nt(w1, h3) + nt(w2, h2) + nt(w3, h1)))
    scores = jax.nn.sigmoid(logits)
    choice = scores + rb_ref[...]
    member = lax.broadcasted_iota(jnp.int32, (GROUP_SIZE, tm), 0)
    blocks, gscore = [], []
    for g in range(N_EXPERT_GROUPS):
        blk = choice[g * GROUP_SIZE:(g + 1) * GROUP_SIZE]
        m1 = jnp.max(blk, axis=0, keepdims=True)
        first = jnp.min(jnp.where(blk == m1, member, GROUP_SIZE), axis=0, keepdims=True)
        m2 = jnp.max(jnp.where(member == first, -jnp.inf, blk), axis=0, keepdims=True)
        blocks.append(blk)
        gscore.append(m1 + m2)
    masked = []
    for g in range(N_EXPERT_GROUPS):
        rank = jnp.zeros((1, tm), jnp.int32)
        for g2 in range(N_EXPERT_GROUPS):
            if g2 == g:
                continue
            beats = (gscore[g2] > gscore[g]) | ((gscore[g2] == gscore[g]) & (g2 < g))
            rank = rank + beats.astype(jnp.int32)
        masked.append(jnp.where(rank < TOPK_GROUPS, blocks[g], NEG_INF))
    vm = jnp.concatenate(masked, axis=0)
    eidx = lax.broadcasted_iota(jnp.int32, (N_EXPERTS, tm), 0)
    cnt = jnp.zeros((N_EXPERTS, tm), jnp.int32)
    for e2 in range(N_EXPERTS):
        r = vm[e2:e2 + 1]
        beats = (r > vm) | ((r == vm) & (e2 < eidx))
        cnt = cnt + beats.astype(jnp.int32)
    w = jnp.where(cnt < TOP_K, scores, 0.0)
    gt_ref[...] = w / (jnp.sum(w, axis=0, keepdims=True) + 1e-20) * ROUTED_SCALE


def _merge(x2d, od, of, wo_bf, g, b, wrt, rb, alpha, tm):
    n, d = x2d.shape
    row = lambda w: pl.BlockSpec((tm, w), lambda r: (r, 0))
    full = lambda a: pl.BlockSpec(a.shape, lambda r: (0,) * a.ndim)
    return pl.pallas_call(
        functools.partial(_merge_kernel, alpha=alpha),
        grid=(n // tm,),
        in_specs=[row(d), row(DIFF_W), row(FOX_W), full(wo_bf), full(g), full(b), full(wrt), full(rb)],
        out_specs=[row(d), row(d), pl.BlockSpec((N_EXPERTS, tm), lambda r: (0, r))],
        out_shape=[jax.ShapeDtypeStruct((n, d), F32), jax.ShapeDtypeStruct((n, d), BF16),
                   jax.ShapeDtypeStruct((N_EXPERTS, n), F32)],
        compiler_params=_cparams(("parallel",)),
        name="merge_ln_router",
    )(x2d, od, of, wo_bf, g, b, wrt, rb)


def _swiglu(hb, wgu, f):
    au = jnp.dot(hb, wgu, preferred_element_type=F32)
    return jax.nn.silu(au[:, :f]) * au[:, f:]


def _moe_kernel(hb_ref, h_ref, g_ref, wgu_ref, wd_ref, wsgu_ref, wsd_ref, ln_g_ref, ln_b_ref, y_ref, acc_ref,
                *, alpha, sub):
    e = pl.program_id(1)
    f = wd_ref.shape[1]
    tn = hb_ref.shape[0]

    @pl.when(e == 0)
    def _():
        fs = wsd_ref.shape[0]
        acc_ref[...] = jnp.dot(_swiglu(hb_ref[...], wsgu_ref[...], fs).astype(BF16), wsd_ref[...],
                               preferred_element_type=F32)

    for r0 in range(0, tn, sub):
        gates = g_ref[r0:r0 + sub, :]
        lane = lax.broadcasted_iota(jnp.int32, gates.shape, 1)
        gcol = jnp.sum(jnp.where(lane == e, gates, 0.0), axis=1, keepdims=True)
        act = (_swiglu(hb_ref[r0:r0 + sub, :], wgu_ref[0], f) * gcol).astype(BF16)
        acc_ref[r0:r0 + sub, :] += jnp.dot(act, wd_ref[0], preferred_element_type=F32)

    @pl.when(e == pl.num_programs(1) - 1)
    def _():
        y_ref[...] = _layer_norm(alpha * h_ref[...] + acc_ref[...], ln_g_ref[...], ln_b_ref[...])


def _moe(hb, h, gates, wgu, wd, wsgu, wsd, ln_g, ln_b, alpha, tn):
    n, d = h.shape
    ne = wgu.shape[0]
    row = lambda w: pl.BlockSpec((tn, w), lambda i, e: (i, 0))
    full = lambda a: pl.BlockSpec(a.shape, lambda i, e: (0,) * a.ndim)
    return pl.pallas_call(
        functools.partial(_moe_kernel, alpha=alpha, sub=min(tn, MOE_SUB_ROWS)),
        grid=(n // tn, ne),
        in_specs=[row(d), row(d), row(ne),
                  pl.BlockSpec((1,) + wgu.shape[1:], lambda i, e: (e, 0, 0)),
                  pl.BlockSpec((1,) + wd.shape[1:], lambda i, e: (e, 0, 0)),
                  full(wsgu), full(wsd), full(ln_g), full(ln_b)],
        out_specs=row(d),
        out_shape=jax.ShapeDtypeStruct((n, d), F32),
        scratch_shapes=[pltpu.VMEM((tn, d), F32)],
        compiler_params=_cparams(("parallel", "arbitrary")),
        name="moe_ln",
    )(hb, h, gates, wgu, wd, wsgu, wsd, ln_g, ln_b)


def _pick_tile(n, pref):
    t = min(n, pref)
    assert n % t == 0
    return t


def _feature_major(a):
    n, tokens = a.shape[:2]
    return jnp.moveaxis(a.reshape(n, tokens, -1), 1, 2)


def _token_major(a, feat_shape):
    n, _, tokens = a.shape
    return jnp.moveaxis(a, 1, 2).reshape((n, tokens) + feat_shape)


def kernel(x_prompt, x_sample, cache_diff_k, cache_diff_v, cache_fox_k, cache_fox_v, cache_fox_logf, page_table,
           w_in, b_forget, lambda_q1, lambda_k1, lambda_q2, lambda_k2, subln_gain, w_o, ln1_g, ln1_b, w_router,
           router_bias, w_exp_gate, w_exp_up, w_exp_down, w_sh_gate, w_sh_up, w_sh_down, ln2_g, ln2_b):
    depth = w_in.shape[0]
    nb, seq, d_model = x_prompt.shape
    dec_b, dec_seq, _ = x_sample.shape
    n_pages = page_table.shape[1]
    page = cache_diff_k.shape[2]
    past_len = n_pages * page
    alpha = (2.0 * depth) ** 0.25
    n_s = dec_b * dec_seq

    tq = _pick_tile(seq, 256)
    tm_p = _pick_tile(seq, 256)
    tabs_p = _rope_tables(seq, 0, seq)
    tabs_s = _rope_tables(n_s, past_len, dec_seq)

    xp = x_prompt.reshape(nb * seq, d_model)
    xs = x_sample.reshape(n_s, d_model)
    outs_p, outs_s = [], []
    for l in range(depth):
        lam_init = 0.8 - 0.6 * math.exp(-0.3 * l)
        w = w_in[l]
        cuts = [0, DIFF_W, 2 * DIFF_W, 3 * DIFF_W, 3 * DIFF_W + FOX_W, 3 * DIFF_W + 2 * FOX_W, 3 * DIFF_W + 3 * FOX_W]
        wqd, wkd, wvd, wqf, wkf, wvf = [w[:, cuts[i]:cuts[i + 1]] for i in range(6)]
        wfl = w[:, cuts[6]:]
        pw = {
            "wt": jnp.concatenate([wqd, wqf, wkd, wkf, wvd, wvf], axis=1).T.astype(BF16),
            "wk": jnp.concatenate([wkd, wkf], axis=1).astype(BF16),
            "wtfl": jnp.pad(wfl.T, ((0, BF16_ROWS - N_FOX_HEADS), (0, 0))).astype(BF16),
            "btfl": b_forget[l][:, None],
        }
        lambdas = [v[l][None, :] for v in (lambda_q1, lambda_k1, lambda_q2, lambda_k2)]
        gain = subln_gain[l]
        gain128 = jnp.tile(gain, LANES // HEAD_DIM)[None, :]
        gain512 = jnp.tile(gain, DIFF_W // HEAD_DIM)[None, :]
        wo_bf = w_o[l].astype(BF16)
        wrt = w_router[l].T
        rb = router_bias[l][:, None]
        wgu = jnp.concatenate([w_exp_gate[l], w_exp_up[l]], axis=2).astype(BF16)
        wd = w_exp_down[l].astype(BF16)
        wsgu = jnp.concatenate([w_sh_gate[l], w_sh_up[l]], axis=1).astype(BF16)
        wsd = w_sh_down[l].astype(BF16)
        g1, b1 = ln1_g[l][None, :], ln1_b[l][None, :]
        g2, b2 = ln2_g[l][None, :], ln2_b[l][None, :]

        def ffn(x2d, od, of, tm, tn):
            h, hb, gt = _merge(x2d, od, of, wo_bf, g1, b1, wrt, rb, alpha, tm)
            return _moe(hb, h, gt.T, wgu, wd, wsgu, wsd, g2, b2, alpha, tn)

        (qdt, qft, kdt, kft, vdt, vft, kdb, kfb, v1d, v1f, logft) = _project(xp, nb, seq, pw, tabs_p, tm_p, tq)
        ct, terms = _cumsum_t(logft)
        n_pairs = N_FOX_HEADS // 2
        cqt_pairs = ct.reshape(nb, n_pairs, 2, seq)
        kbias_pairs = jnp.transpose(terms.reshape(nb, 3, n_pairs, 2, seq), (0, 2, 4, 3, 1)).reshape(nb, n_pairs, seq, 6)
        kbias_pairs = jnp.pad(kbias_pairs, ((0, 0), (0, 0), (0, 0), (0, LANES - 6))).astype(BF16)
        od, of = _prompt_attention(qdt, kdb.reshape(nb, seq, DIFF_W), v1d, qft, kfb.reshape(nb, seq, FOX_W), v1f,
                                   kbias_pairs, cqt_pairs, lambdas, gain128, lam_init, tq)
        xp_new = ffn(xp, od.reshape(nb * seq, DIFF_W), of.reshape(nb * seq, FOX_W),
                     _pick_tile(nb * seq, 256), _pick_tile(nb * seq, 1024))
        outs_p.append((kdt, vdt, kft, vft, logft))

        (sqdt, sqft, skdt, skft, svdt, svft, _, _, _, _, slogft) = _project(xs, 1, n_s, pw, tabs_s, n_s, n_s)
        qd_rows = sqdt[0].T.reshape(dec_b, dec_seq, 2 * N_DIFF_HEADS, DIFF_D)
        eye_hm = jnp.eye(2 * N_DIFF_HEADS, dtype=BF16)
        qd_bd = (qd_rows[:, :, :, None, :] * eye_hm[None, None, :, :, None]).reshape(
            dec_b, dec_seq * 2 * N_DIFF_HEADS, DIFF_W)
        qf_rows = sqft[0].T.reshape(dec_b, dec_seq, N_FOX_HEADS, HEAD_DIM)
        eye_h = jnp.eye(N_FOX_HEADS, dtype=BF16)
        qf_bd = (qf_rows[:, :, :, None, :] * eye_h[None, None, :, :, None]).reshape(
            dec_b, dec_seq * N_FOX_HEADS, FOX_W)
        caches = [_feature_major(c[l]) for c in (cache_diff_k, cache_diff_v, cache_fox_k, cache_fox_v, cache_fox_logf)]
        self_page = lambda a: jnp.pad(jnp.transpose(a[0].reshape(-1, dec_b, dec_seq), (1, 0, 2)),
                                      ((0, 0), (0, 0), (0, page - dec_seq)))
        slf_t = jnp.tile(self_page(slogft), (1, dec_seq, 1))
        selfs = [self_page(skdt), self_page(svdt), self_page(skft), self_page(svft), slf_t]
        od_s, of_s = _sample_attention(page_table, qd_bd, qf_bd, caches, selfs, lambdas, gain512, lam_init, dec_seq)
        od_s = od_s[:, :dec_seq].reshape(n_s, DIFF_W)
        of_s = of_s[:, :dec_seq].reshape(n_s, FOX_W)
        xs_new = ffn(xs, od_s, of_s, n_s, n_s)
        outs_s.append(tuple(a.reshape(a.shape[1], dec_b, dec_seq) for a in (skdt, svdt, skft, svft, slogft)))
        xp, xs = xp_new, xs_new

    def stack_p(idx, feat_shape):
        return jnp.stack([_token_major(o[idx], feat_shape) for o in outs_p])

    def stack_s(idx, feat_shape):
        return jnp.stack([jnp.transpose(o[idx], (1, 2, 0)).reshape((dec_b, dec_seq) + feat_shape) for o in outs_s])

    shapes = ((N_DIFF_HEADS, 2, DIFF_D), (N_DIFF_HEADS, HEAD_DIM), (N_FOX_HEADS, HEAD_DIM), (N_FOX_HEADS, HEAD_DIM),
              (N_FOX_HEADS,))
    return (xp.reshape(nb, seq, d_model), xs.reshape(dec_b, dec_seq, d_model),
            *[stack_p(i, s) for i, s in enumerate(shapes)], *[stack_s(i, s) for i, s in enumerate(shapes)])
```

```python
import functools
import math

import jax
import jax.numpy as jnp
from jax import lax
from jax.experimental import pallas as pl
from jax.experimental.pallas import tpu as pltpu

F32 = jnp.float32
BF16 = jnp.bfloat16

HEAD_DIM = 64
N_DIFF_HEADS = 8
N_FOX_HEADS = 8
DIFF_D = HEAD_DIM // 2
DIFF_W = N_DIFF_HEADS * HEAD_DIM
FOX_W = N_FOX_HEADS * HEAD_DIM
ROT_DIM = DIFF_D // 4
ROT_HALF = ROT_DIM // 2
ROPE_THETA = 500000.0
N_EXPERTS = 64
N_EXPERT_GROUPS = 8
GROUP_SIZE = N_EXPERTS // N_EXPERT_GROUPS
TOPK_GROUPS = 4
TOP_K = 8
ROUTED_SCALE = 2.5
LN_EPS = 1e-5
RMS_EPS = 1e-5
NEG_INF = -1e30
LOG2E = math.log2(math.e)

LANES = 128
SUBLANES = 8
BF16_ROWS = 16
V_ROWS = HEAD_DIM + BF16_ROWS
VMEM_LIMIT = 56 * 1024 * 1024
MOE_SUB_ROWS = 512
N_SCORE_BUFS = 4

NT_DIMS = (((1,), (1,)), ((), ()))


def _cparams(sem):
    return pltpu.CompilerParams(dimension_semantics=sem, vmem_limit_bytes=VMEM_LIMIT)


def _log_sigmoid(x):
    return jnp.minimum(x, 0.0) - jnp.log1p(jnp.exp(-jnp.abs(x)))


def _layer_norm(y, g, b):
    mu = jnp.mean(y, axis=-1, keepdims=True)
    d = y - mu
    var = jnp.mean(d * d, axis=-1, keepdims=True)
    return d * lax.rsqrt(var + LN_EPS) * g + b


def _split3(a):
    a1 = a.astype(BF16)
    r1 = a - a1.astype(F32)
    a2 = r1.astype(BF16)
    a3 = (r1 - a2.astype(F32)).astype(BF16)
    return a1, a2, a3


def _dot_sel_lhs(sel, b):
    return sum(jnp.dot(sel, t, preferred_element_type=F32) for t in _split3(b))


def _dot_sel_rhs(a, sel):
    return sum(jnp.dot(t, sel, preferred_element_type=F32) for t in _split3(a))


def _lambda_full(lq1_ref, lk1_ref, lq2_ref, lk2_ref, lam_init):
    a = jnp.sum(lq1_ref[...] * lk1_ref[...], axis=1, keepdims=True)
    b = jnp.sum(lq2_ref[...] * lk2_ref[...], axis=1, keepdims=True)
    return jnp.exp(a) - jnp.exp(b) + lam_init


def _rope_table_kernel(invf_lane_ref, invf_sub_ref, c_ref, s1_ref, s2_ref, ct_ref, st_ref, *, base, mod, tm):
    i = pl.program_id(0)
    row = lax.broadcasted_iota(jnp.int32, (tm, LANES), 0) + i * tm
    pos = (base + row % mod).astype(F32)
    ang = pos * invf_lane_ref[...]
    li = lax.broadcasted_iota(jnp.int32, (tm, LANES), 1) % DIFF_D
    c = jnp.cos(ang)
    s = jnp.sin(ang)
    c_ref[...] = jnp.where(li < ROT_DIM, c, 1.0)
    s1_ref[...] = jnp.where(li < ROT_HALF, -s, 0.0)
    s2_ref[...] = jnp.where((li >= ROT_HALF) & (li < ROT_DIM), s, 0.0)
    col = lax.broadcasted_iota(jnp.int32, (SUBLANES, tm), 1) + i * tm
    pos_t = (base + col % mod).astype(F32)
    ang_t = invf_sub_ref[...] * pos_t
    sub = lax.broadcasted_iota(jnp.int32, (SUBLANES, tm), 0)
    ct_ref[...] = jnp.cos(ang_t)
    st_ref[...] = jnp.where(sub < ROT_HALF, -jnp.sin(ang_t), jnp.sin(ang_t))


def _rope_tables(n_pos, base, mod):
    inv4 = jnp.power(ROPE_THETA, -jnp.arange(ROT_HALF, dtype=F32) * 2.0 / ROT_DIM)
    invf_lane = inv4[(jnp.arange(LANES) % DIFF_D) % ROT_HALF][None, :]
    invf_sub = inv4[jnp.arange(SUBLANES) % ROT_HALF][:, None]
    tm = min(n_pos, 1024)
    assert n_pos % tm == 0
    lane_spec = pl.BlockSpec((tm, LANES), lambda i: (i, 0))
    sub_spec = pl.BlockSpec((SUBLANES, tm), lambda i: (0, i))
    return pl.pallas_call(
        functools.partial(_rope_table_kernel, base=base, mod=mod, tm=tm),
        grid=(n_pos // tm,),
        in_specs=[pl.BlockSpec((1, LANES), lambda i: (0, 0)), pl.BlockSpec((SUBLANES, 1), lambda i: (0, 0))],
        out_specs=[lane_spec, lane_spec, lane_spec, sub_spec, sub_spec],
        out_shape=[jax.ShapeDtypeStruct((n_pos, LANES), F32)] * 3 + [jax.ShapeDtypeStruct((SUBLANES, n_pos), F32)] * 2,
        compiler_params=_cparams(("parallel",)),
        name="rope_tables",
    )(invf_lane, invf_sub)


def _rope_sublanes(zt, ct, st):
    rows = []
    for j in range(DIFF_W // DIFF_D):
        x8 = zt[DIFF_D * j:DIFF_D * j + ROT_DIM]
        rows.append(x8 * ct + pltpu.roll(x8, ROT_HALF, 0) * st)
        rows.append(zt[DIFF_D * j + ROT_DIM:DIFF_D * (j + 1)])
    return jnp.concatenate(rows, axis=0)


def _proj_kernel(x_ref, wt_ref, wk_ref, wtfl_ref, btfl_ref, c_ref, s1_ref, s2_ref, ct_ref, st_ref,
                 qdt_ref, qft_ref, kdt_ref, kft_ref, vdt_ref, vft_ref, kdb_ref, kfb_ref, v1d_ref, v1f_ref, logft_ref,
                 *, tm, tk):
    xb = x_ref[...].astype(BF16)
    zt = lax.dot_general(wt_ref[...], xb, NT_DIMS, preferred_element_type=F32)
    ct, st = ct_ref[...], st_ref[...]
    w = DIFF_W
    qdt_ref[0] = (_rope_sublanes(zt[0:w], ct, st) * (DIFF_D ** -0.5 * LOG2E)).astype(BF16)
    qft_ref[0] = (zt[w:2 * w] * (HEAD_DIM ** -0.5 * LOG2E)).astype(BF16)
    kdt_ref[0] = _rope_sublanes(zt[2 * w:3 * w], ct, st)
    kft_ref[0] = zt[3 * w:4 * w]
    vd = zt[4 * w:5 * w]
    vf = zt[5 * w:6 * w]
    vdt_ref[0] = vd
    vft_ref[0] = vf
    ones = jnp.ones((BF16_ROWS, tk), BF16)
    for h in range(N_DIFF_HEADS):
        for t in range(tm // tk):
            v1d_ref[0, h, t, 0:HEAD_DIM, :] = vd[HEAD_DIM * h:HEAD_DIM * (h + 1), t * tk:(t + 1) * tk].astype(BF16)
            v1d_ref[0, h, t, HEAD_DIM:V_ROWS, :] = ones
            v1f_ref[0, h, t, 0:HEAD_DIM, :] = vf[HEAD_DIM * h:HEAD_DIM * (h + 1), t * tk:(t + 1) * tk].astype(BF16)
            v1f_ref[0, h, t, HEAD_DIM:V_ROWS, :] = ones
    flt = lax.dot_general(wtfl_ref[...], xb, NT_DIMS, preferred_element_type=F32)
    logft_ref[0] = _log_sigmoid(flt[:N_FOX_HEADS] + btfl_ref[...])

    zk = jnp.dot(xb, wk_ref[...], preferred_element_type=F32)
    c, s1, s2 = c_ref[...], s1_ref[...], s2_ref[...]
    pieces = []
    for j in range(DIFF_W // LANES):
        xk = zk[:, j * LANES:(j + 1) * LANES]
        pieces.append(xk * c + pltpu.roll(xk, LANES - ROT_HALF, 1) * s1 + pltpu.roll(xk, ROT_HALF, 1) * s2)
    kdb_ref[...] = jnp.concatenate(pieces, axis=1).astype(BF16)
    kfb_ref[...] = zk[:, DIFF_W:].astype(BF16)


def _project(x2d, n_batch, seq, w, tables, tm, tk):
    n, d_model = x2d.shape
    nt = seq // tm
    c, s1, s2, ct, st = tables
    full = lambda a: pl.BlockSpec(a.shape, lambda r: (0,) * a.ndim)
    row512 = pl.BlockSpec((tm, DIFF_W), lambda r: (r, 0))
    tab = pl.BlockSpec((tm, LANES), lambda r: (r % nt, 0))
    tabt = pl.BlockSpec((SUBLANES, tm), lambda r: (0, r % nt))
    ft_spec = pl.BlockSpec((1, DIFF_W, tm), lambda r: (r // nt, 0, r % nt))
    v1_spec = pl.BlockSpec((1, N_DIFF_HEADS, tm // tk, V_ROWS, tk), lambda r: (r // nt, 0, r % nt, 0, 0))
    lt_spec = pl.BlockSpec((1, N_FOX_HEADS, tm), lambda r: (r // nt, 0, r % nt))
    sds = jax.ShapeDtypeStruct
    return pl.pallas_call(
        functools.partial(_proj_kernel, tm=tm, tk=tk),
        grid=(n // tm,),
        in_specs=[pl.BlockSpec((tm, d_model), lambda r: (r, 0)),
                  full(w["wt"]), full(w["wk"]), full(w["wtfl"]), full(w["btfl"]), tab, tab, tab, tabt, tabt],
        out_specs=[ft_spec] * 6 + [row512, row512, v1_spec, v1_spec, lt_spec],
        out_shape=[sds((n_batch, DIFF_W, seq), BF16)] * 2 + [sds((n_batch, DIFF_W, seq), F32)] * 4
                  + [sds((n, DIFF_W), BF16)] * 2
                  + [sds((n_batch, N_DIFF_HEADS, seq // tk, V_ROWS, tk), BF16)] * 2
                  + [sds((n_batch, N_FOX_HEADS, seq), F32)],
        compiler_params=_cparams(("parallel",)),
        name="in_proj",
    )(x2d, w["wt"], w["wk"], w["wtfl"], w["btfl"], c, s1, s2, ct, st)


def _lane_cumsum(x):
    lane = lax.broadcasted_iota(jnp.int32, x.shape, 1)
    s = 1
    while s < LANES:
        x = x + jnp.where(lane >= s, pltpu.roll(x, s, 1), 0.0)
        s *= 2
    return x


def _lane_rev_cumsum(x):
    lane = lax.broadcasted_iota(jnp.int32, x.shape, 1)
    s = 1
    while s < LANES:
        x = x + jnp.where(lane < LANES - s, pltpu.roll(x, LANES - s, 1), 0.0)
        s *= 2
    return x


def _cumsum_kernel(lft_ref, ct_ref, terms_ref, *, seq):
    carry = jnp.zeros((N_FOX_HEADS, 1), F32)
    for t in range(seq // LANES):
        x = _lane_cumsum(lft_ref[0, :, t * LANES:(t + 1) * LANES]) + carry
        ct_ref[0, :, t * LANES:(t + 1) * LANES] = x
        for n, term in enumerate(_split3(-LOG2E * x)):
            terms_ref[0, n, :, t * LANES:(t + 1) * LANES] = term.astype(F32)
        carry = x[:, LANES - 1:LANES]


def _cumsum_t(logft):
    nb, nh, seq = logft.shape
    spec = pl.BlockSpec((1, nh, seq), lambda b: (b, 0, 0))
    return pl.pallas_call(
        functools.partial(_cumsum_kernel, seq=seq),
        grid=(nb,), in_specs=[spec],
        out_specs=[spec, pl.BlockSpec((1, 3, nh, seq), lambda b: (b, 0, 0, 0))],
        out_shape=[jax.ShapeDtypeStruct(logft.shape, F32), jax.ShapeDtypeStruct((nb, 3, nh, seq), F32)],
        compiler_params=_cparams(("parallel",)),
        name="logf_cumsum",
    )(logft)


def _block_diag_q(qt, nblk):
    rows_per = LANES // nblk
    rb = lax.broadcasted_iota(jnp.int32, qt.shape, 0) // rows_per
    zero = jnp.zeros_like(qt)
    return jnp.concatenate([jnp.where(rb == j, qt, zero) for j in range(nblk)], axis=1)


def _scores(j, k_ref, qbd_ref, s_ref, tk, kbias_ref=None):
    start = pl.multiple_of(j * tk, tk)
    k = k_ref[0, pl.ds(start, tk), :]
    if kbias_ref is not None:
        k = jnp.concatenate([k, kbias_ref[0, 0, pl.ds(start, tk), :]], axis=1)
    s_ref[...] = jnp.dot(k, qbd_ref[...], preferred_element_type=F32)


def _softmax_pv(j, s_ref, vt_ref, acc_ref, m_ref, *, tq, nblk, blk_head, masked, cq=None):
    tk = tq
    if masked:
        keep = (lax.broadcasted_iota(jnp.int32, (tk, tq), 0) <= lax.broadcasted_iota(jnp.int32, (tk, tq), 1))
    for blk in range(nblk):
        sb = s_ref[:, blk * tq:(blk + 1) * tq]
        if masked:
            sb = jnp.where(keep, sb, NEG_INF)
        m_old = m_ref[blk, 0:1, :]
        smax = jnp.max(sb, axis=0, keepdims=True)
        if cq is not None:
            m_new = jnp.maximum(m_old, smax + cq[blk])
            shift = m_new - cq[blk]
        else:
            m_new = jnp.maximum(m_old, smax)
            shift = m_new
        p = jnp.exp2(sb - shift).astype(BF16)
        alpha = jnp.exp2(m_old - m_new)
        vt = vt_ref[0, blk_head[blk], j]
        acc_ref[blk] = acc_ref[blk] * alpha + jnp.dot(vt, p, preferred_element_type=F32)
        m_ref[blk] = jnp.broadcast_to(m_new, (SUBLANES, tq))


def _causal_sweep(i, scores, update):
    scores(0, 0)

    def body(jj, carry):
        j = N_SCORE_BUFS * jj
        for r in range(N_SCORE_BUFS):
            scores(j + r + 1, (r + 1) % N_SCORE_BUFS)
            update(j + r, r, False)
        return carry

    lax.fori_loop(0, i // N_SCORE_BUFS, body, 0)
    rem = i % N_SCORE_BUFS
    j0 = i - rem
    for r in range(N_SCORE_BUFS - 1):
        @pl.when(r < rem)
        def _():
            scores(j0 + r + 1, r + 1)
            update(j0 + r, r, False)

    for r in range(N_SCORE_BUFS):
        @pl.when(rem == r)
        def _():
            update(i, r, True)


def _attn_init(acc_ref, m_ref):
    acc_ref[...] = jnp.zeros(acc_ref.shape, F32)
    m_ref[...] = jnp.full(m_ref.shape, NEG_INF, F32)


def _diff_attn_kernel(lq1_ref, lk1_ref, lq2_ref, lk2_ref, gain_ref, qt_ref, k_ref, vt_ref, o_ref,
                      acc_ref, m_ref, qbd_ref, *s_refs, tq, lam_init):
    i = pl.program_id(2)
    nblk = 4
    qbd_ref[...] = _block_diag_q(qt_ref[0], nblk)
    _attn_init(acc_ref, m_ref)
    scores = lambda j, slot: _scores(j, k_ref, qbd_ref, s_refs[slot], tq)
    update = lambda j, slot, masked: _softmax_pv(j, s_refs[slot], vt_ref, acc_ref, m_ref, tq=tq, nblk=nblk,
                                                 blk_head=(0, 0, 1, 1), masked=masked)
    _causal_sweep(i, scores, update)

    lam = _lambda_full(lq1_ref, lk1_ref, lq2_ref, lk2_ref, lam_init)
    outs = []
    for hh in range(2):
        a1 = acc_ref[2 * hh]
        a2 = acc_ref[2 * hh + 1]
        o = a1[0:HEAD_DIM] / a1[HEAD_DIM:HEAD_DIM + 1] - lam * (a2[0:HEAD_DIM] / a2[HEAD_DIM:HEAD_DIM + 1])
        ms = jnp.mean(o * o, axis=0, keepdims=True)
        outs.append(o * lax.rsqrt(ms + RMS_EPS))
    o = jnp.concatenate(outs, axis=0).T
    o_ref[0] = (o * gain_ref[...] * (1.0 - lam_init)).astype(o_ref.dtype)


def _fox_attn_kernel(qt_ref, k_ref, vt_ref, kbias_ref, cqt_ref, o_ref, acc_ref, m_ref, qbd_ref, *s_refs, tq):
    i = pl.program_id(2)
    nblk = 2
    qbd_ref[0:LANES, :] = _block_diag_q(qt_ref[0], nblk)
    r = lax.broadcasted_iota(jnp.int32, (LANES, nblk * tq), 0)
    cb = lax.broadcasted_iota(jnp.int32, (LANES, nblk * tq), 1) // tq
    qbd_ref[LANES:2 * LANES, :] = jnp.where((r < 3 * nblk) & (r // 3 == cb), 1.0, 0.0).astype(BF16)
    _attn_init(acc_ref, m_ref)
    cq = [cqt_ref[0, 0, hh:hh + 1, :] * LOG2E for hh in range(nblk)]
    scores = lambda j, slot: _scores(j, k_ref, qbd_ref, s_refs[slot], tq, kbias_ref)
    update = lambda j, slot, masked: _softmax_pv(j, s_refs[slot], vt_ref, acc_ref, m_ref, tq=tq, nblk=nblk,
                                                 blk_head=(0, 1), masked=masked, cq=cq)
    _causal_sweep(i, scores, update)
    outs = []
    for hh in range(nblk):
        a = acc_ref[hh]
        outs.append(a[0:HEAD_DIM] / a[HEAD_DIM:HEAD_DIM + 1])
    o_ref[0] = jnp.concatenate(outs, axis=0).T.astype(o_ref.dtype)


def _prompt_attention(qdt, kdb, v1d, qft, kfb, v1f, kbias_pairs, cqt_pairs, lambdas, gain128, lam_init, tq):
    nb, _, seq = qdt.shape
    n_pairs = DIFF_W // LANES
    grid = (nb, n_pairs, seq // tq)
    qt_spec = pl.BlockSpec((1, LANES, tq), lambda b, p, i: (b, p, i))
    k_spec = pl.BlockSpec((1, seq, LANES), lambda b, p, i: (b, 0, p))
    vt_spec = pl.BlockSpec((1, 2, seq // tq, V_ROWS, tq), lambda b, p, i: (b, p, 0, 0, 0))
    o_spec = pl.BlockSpec((1, tq, LANES), lambda b, p, i: (b, i, p))
    small = lambda a: pl.BlockSpec(a.shape, lambda b, p, i: (0,) * a.ndim)
    sem = ("parallel", "parallel", "arbitrary")

    def scratch(nblk, k_width):
        return [pltpu.VMEM((nblk, V_ROWS, tq), F32), pltpu.VMEM((nblk, SUBLANES, tq), F32),
                pltpu.VMEM((k_width, nblk * tq), BF16)] + [pltpu.VMEM((tq, nblk * tq), F32)] * N_SCORE_BUFS

    od = pl.pallas_call(
        functools.partial(_diff_attn_kernel, tq=tq, lam_init=lam_init),
        grid=grid,
        in_specs=[small(lambdas[0])] * 4 + [small(gain128), qt_spec, k_spec, vt_spec],
        out_specs=o_spec,
        out_shape=jax.ShapeDtypeStruct((nb, seq, DIFF_W), BF16),
        scratch_shapes=scratch(4, LANES),
        compiler_params=_cparams(sem),
        name="diff_attention",
    )(*lambdas, gain128, qdt, kdb, v1d)
    kbias_spec = pl.BlockSpec((1, 1, seq, LANES), lambda b, p, i: (b, p, 0, 0))
    cq_spec = pl.BlockSpec((1, 1, 2, tq), lambda b, p, i: (b, p, 0, i))
    of = pl.pallas_call(
        functools.partial(_fox_attn_kernel, tq=tq),
        grid=grid,
        in_specs=[qt_spec, k_spec, vt_spec, kbias_spec, cq_spec],
        out_specs=o_spec,
        out_shape=jax.ShapeDtypeStruct((nb, seq, FOX_W), BF16),
        scratch_shapes=scratch(2, 2 * LANES),
        compiler_params=_cparams(sem),
        name="fox_attention",
    )(qft, kfb, v1f, kbias_pairs, cqt_pairs)
    return od, of


def _row_softmax_update(s, vts, m_ref, l_ref, acc_ref):
    m_old = m_ref[:, 0:1]
    m_new = jnp.maximum(m_old, jnp.max(s, axis=1, keepdims=True))
    p = jnp.exp2(s - m_new)
    alpha = jnp.exp2(m_old - m_new)
    l_new = alpha * l_ref[:, 0:1] + jnp.sum(p, axis=1, keepdims=True)
    pb = p.astype(BF16)
    pv = sum(lax.dot_general(pb[:, i * LANES:(i + 1) * LANES], vt.astype(BF16), NT_DIMS, preferred_element_type=F32)
             for i, vt in enumerate(vts))
    acc_ref[...] = alpha * acc_ref[...] + pv
    m_ref[...] = jnp.broadcast_to(m_new, m_ref.shape)
    l_ref[...] = jnp.broadcast_to(l_new, l_ref.shape)


def _sample_attn_kernel(pt_ref, lq1_ref, lk1_ref, lq2_ref, lk2_ref, gain_ref, qd_ref, qf_ref, *rest,
                        dec_seq, lam_init, pages_per_step):
    pp = pages_per_step
    dk_refs, dv_refs, fk_refs, fv_refs, lf_refs = (rest[i * pp:(i + 1) * pp] for i in range(5))
    (sdk_ref, sdv_ref, sfk_ref, sfv_ref, slf_ref, od_ref, of_ref,
     md_ref, ld_ref, accd_ref, mf_ref, lfs_ref, accf_ref, carry_ref, cself_ref) = rest[5 * pp:]
    j = pl.program_id(1)
    rd = dec_seq * 2 * N_DIFF_HEADS
    rf = dec_seq * N_FOX_HEADS
    lane_d = lax.broadcasted_iota(jnp.int32, (rd, LANES), 1)
    t_d = lax.broadcasted_iota(jnp.int32, (rd, LANES), 0) // (2 * N_DIFF_HEADS)
    lane_f = lax.broadcasted_iota(jnp.int32, (rf, LANES), 1)
    t_f = lax.broadcasted_iota(jnp.int32, (rf, LANES), 0) // N_FOX_HEADS

    def scores(q_ref, kts):
        q = q_ref[0]
        return jnp.concatenate([jnp.dot(q, kt.astype(BF16), preferred_element_type=F32) for kt in kts], axis=1)

    @pl.when(j == 0)
    def _():
        md_ref[...] = jnp.full(md_ref.shape, NEG_INF, F32)
        ld_ref[...] = jnp.zeros(ld_ref.shape, F32)
        accd_ref[...] = jnp.zeros(accd_ref.shape, F32)
        mf_ref[...] = jnp.full(mf_ref.shape, NEG_INF, F32)
        lfs_ref[...] = jnp.zeros(lfs_ref.shape, F32)
        accf_ref[...] = jnp.zeros(accf_ref.shape, F32)
        carry_ref[...] = jnp.zeros(carry_ref.shape, F32)
        sd = jnp.where((lane_d < dec_seq) & (lane_d <= t_d), scores(qd_ref, [sdk_ref[0]]), NEG_INF)
        _row_softmax_update(sd, [sdv_ref[0]], md_ref, ld_ref, accd_ref)
        incl = _lane_cumsum(slf_ref[0])
        cs_t = jnp.sum(jnp.where(lane_f == t_f, incl, 0.0), axis=1, keepdims=True)
        cself_ref[...] = jnp.broadcast_to(cs_t, cself_ref.shape)
        sf = scores(qf_ref, [sfk_ref[0]]) + (cs_t - incl) * LOG2E
        sf = jnp.where((lane_f < dec_seq) & (lane_f <= t_f), sf, NEG_INF)
        _row_softmax_update(sf, [sfv_ref[0]], mf_ref, lfs_ref, accf_ref)

    _row_softmax_update(scores(qd_ref, [r[0] for r in dk_refs]), [r[0] for r in dv_refs], md_ref, ld_ref, accd_ref)
    cself = cself_ref[:, 0:1]
    carry = carry_ref[:, 0:1]
    biases = []
    for r in lf_refs:
        lf = jnp.concatenate([r[0]] * dec_seq, axis=0)
        suffix = _lane_rev_cumsum(lf)
        biases.append(cself + carry + (suffix - lf))
        carry = carry + suffix[:, 0:1]
    carry_ref[...] = jnp.broadcast_to(carry, carry_ref.shape)
    sf = scores(qf_ref, [r[0] for r in fk_refs]) + jnp.concatenate(biases, axis=1) * LOG2E
    _row_softmax_update(sf, [r[0] for r in fv_refs], mf_ref, lfs_ref, accf_ref)

    @pl.when(j == pl.num_programs(1) - 1)
    def _():
        lam = _lambda_full(lq1_ref, lk1_ref, lq2_ref, lk2_ref, lam_init)
        width = accd_ref.shape[1]
        accn = accd_ref[...] / ld_ref[:, 0:1]
        col_h = lax.broadcasted_iota(jnp.int32, (rd, width), 1) // HEAD_DIM
        row = lax.broadcasted_iota(jnp.int32, (rd, width), 0)
        accn = jnp.where(col_h == (row % (2 * N_DIFF_HEADS)) // 2, accn, 0.0)
        sr = lax.broadcasted_iota(jnp.int32, (BF16_ROWS, rd), 0)
        sc = lax.broadcasted_iota(jnp.int32, (BF16_ROWS, rd), 1)
        o_maps = []
        for mp in range(2):
            sel = ((sc // (2 * N_DIFF_HEADS) == sr) & (sc % 2 == mp)).astype(BF16)
            o_maps.append(_dot_sel_lhs(sel, accn))
        od = o_maps[0] - lam * o_maps[1]
        gi = lax.broadcasted_iota(jnp.int32, (width, width), 0) // HEAD_DIM
        gj = lax.broadcasted_iota(jnp.int32, (width, width), 1) // HEAD_DIM
        gmat = jnp.where(gi == gj, 1.0 / HEAD_DIM, 0.0).astype(BF16)
        ms = _dot_sel_rhs(od * od, gmat)
        od_ref[0] = (od * lax.rsqrt(ms + RMS_EPS) * gain_ref[...] * (1.0 - lam_init)).astype(od_ref.dtype)

        accfn = accf_ref[...] / lfs_ref[:, 0:1]
        col_hf = lax.broadcasted_iota(jnp.int32, (rf, width), 1) // HEAD_DIM
        row_f = lax.broadcasted_iota(jnp.int32, (rf, width), 0)
        accfn = jnp.where(col_hf == row_f % N_FOX_HEADS, accfn, 0.0)
        srf = lax.broadcasted_iota(jnp.int32, (BF16_ROWS, rf), 0)
        scf = lax.broadcasted_iota(jnp.int32, (BF16_ROWS, rf), 1)
        self_ = (scf // N_FOX_HEADS == srf).astype(BF16)
        of_ref[0] = _dot_sel_lhs(self_, accfn).astype(of_ref.dtype)


def _sample_attention(page_table, qd_bd, qf_bd, caches, selfs, lambdas, gain512, lam_init, dec_seq):
    nb, n_pages = page_table.shape
    page = caches[0].shape[2]
    assert page == LANES
    rd, rf = qd_bd.shape[1], qf_bd.shape[1]
    pp = next(c for c in (8, 4, 2, 1) if n_pages % c == 0)
    n_steps = n_pages // pp

    def page_map(slot):
        return lambda b, j, pt: (pt[b, n_pages - 1 - (j * pp + slot)], 0, 0)

    per_b = lambda b, j, pt: (b, 0, 0)
    small = lambda a: pl.BlockSpec(a.shape, lambda b, j, pt: (0,) * a.ndim)
    b_spec = lambda a: pl.BlockSpec((1,) + a.shape[1:], per_b)
    out_spec = pl.BlockSpec((1, BF16_ROWS, DIFF_W), per_b)
    cache_specs, cache_args = [], []
    for a in caches:
        for slot in range(pp):
            cache_specs.append(pl.BlockSpec((1,) + a.shape[1:], page_map(slot)))
            cache_args.append(a)
    grid_spec = pltpu.PrefetchScalarGridSpec(
        num_scalar_prefetch=1,
        grid=(nb, n_steps),
        in_specs=[small(lambdas[0])] * 4 + [small(gain512), b_spec(qd_bd), b_spec(qf_bd)]
                 + cache_specs + [b_spec(a) for a in selfs],
        out_specs=[out_spec, out_spec],
        scratch_shapes=[pltpu.VMEM((rd, LANES), F32), pltpu.VMEM((rd, LANES), F32), pltpu.VMEM((rd, DIFF_W), F32),
                        pltpu.VMEM((rf, LANES), F32), pltpu.VMEM((rf, LANES), F32), pltpu.VMEM((rf, FOX_W), F32),
                        pltpu.VMEM((rf, LANES), F32), pltpu.VMEM((rf, LANES), F32)],
    )
    return pl.pallas_call(
        functools.partial(_sample_attn_kernel, dec_seq=dec_seq, lam_init=lam_init, pages_per_step=pp),
        grid_spec=grid_spec,
        out_shape=[jax.ShapeDtypeStruct((nb, BF16_ROWS, DIFF_W), BF16)] * 2,
        compiler_params=_cparams(("parallel", "arbitrary")),
        name="sample_attention",
    )(page_table, *lambdas, gain512, qd_bd, qf_bd, *cache_args, *selfs)


def _merge_kernel(x_ref, od_ref, of_ref, wo_ref, g_ref, b_ref, wr_ref, rb_ref, h_ref, hb_ref, gt_ref, *, alpha):
    o = jnp.concatenate([od_ref[...], of_ref[...]], axis=1)
    mix = jnp.dot(o, wo_ref[...], preferred_element_type=F32)
    h = _layer_norm(alpha * x_ref[...] + mix, g_ref[...], b_ref[...])
    h_ref[...] = h
    hb_ref[...] = h.astype(BF16)

    tm = h.shape[0]
    h1, h2, _ = _split3(h)
    w1, w2, _ = _split3(wr_ref[...])
    mm = lambda a, b: jnp.dot(a, b, preferred_element_type=F32)
    logits = (mm(h1, w1) + (mm(h1, w2) + mm(h2, w1))).T[:N_EXPERTS]
    scores = jax.nn.sigmoid(logits)
    choice = scores + rb_ref[...]
    member = lax.broadcasted_iota(jnp.int32, (GROUP_SIZE, tm), 0)
    blocks, gscore = [], []
    for g in range(N_EXPERT_GROUPS):
        blk = choice[g * GROUP_SIZE:(g + 1) * GROUP_SIZE]
        m1 = jnp.max(blk, axis=0, keepdims=True)
        first = jnp.min(jnp.where(blk == m1, member, GROUP_SIZE), axis=0, keepdims=True)
        m2 = jnp.max(jnp.where(member == first, -jnp.inf, blk), axis=0, keepdims=True)
        blocks.append(blk)
        gscore.append(m1 + m2)
    masked = []
    for g in range(N_EXPERT_GROUPS):
        rank = jnp.zeros((1, tm), jnp.int32)
        for g2 in range(N_EXPERT_GROUPS):
            if g2 == g:
                continue
            beats = (gscore[g2] > gscore[g]) | ((gscore[g2] == gscore[g]) & (g2 < g))
            rank = rank + beats.astype(jnp.int32)
        masked.append(jnp.where(rank < TOPK_GROUPS, blocks[g], NEG_INF))
    vm = jnp.concatenate(masked, axis=0)
    eidx = lax.broadcasted_iota(jnp.int32, (N_EXPERTS, tm), 0)
    cnt = jnp.zeros((N_EXPERTS, tm), jnp.int32)
    for e2 in range(N_EXPERTS):
        r = vm[e2:e2 + 1]
        beats = (r > vm) | ((r == vm) & (e2 < eidx))
        cnt = cnt + beats.astype(jnp.int32)
    w = jnp.where(cnt < TOP_K, scores, 0.0)
    gt_ref[...] = w / (jnp.sum(w, axis=0, keepdims=True) + 1e-20) * ROUTED_SCALE


def _merge(x2d, od, of, wo_bf, g, b, wr_pad, rb, alpha, tm):
    n, d = x2d.shape
    row = lambda w: pl.BlockSpec((tm, w), lambda r: (r, 0))
    full = lambda a: pl.BlockSpec(a.shape, lambda r: (0,) * a.ndim)
    return pl.pallas_call(
        functools.partial(_merge_kernel, alpha=alpha),
        grid=(n // tm,),
        in_specs=[row(d), row(DIFF_W), row(FOX_W), full(wo_bf), full(g), full(b), full(wr_pad), full(rb)],
        out_specs=[row(d), row(d), pl.BlockSpec((N_EXPERTS, tm), lambda r: (0, r))],
        out_shape=[jax.ShapeDtypeStruct((n, d), F32), jax.ShapeDtypeStruct((n, d), BF16),
                   jax.ShapeDtypeStruct((N_EXPERTS, n), F32)],
        compiler_params=_cparams(("parallel",)),
        name="merge_ln_router",
    )(x2d, od, of, wo_bf, g, b, wr_pad, rb)


def _swiglu(hb, wg, wu):
    a = jnp.dot(hb, wg, preferred_element_type=F32)
    u = jnp.dot(hb, wu, preferred_element_type=F32)
    return jax.nn.silu(a) * u


def _moe_kernel(hb_ref, h_ref, g_ref, wg_ref, wu_ref, wd_ref, wsg_ref, wsu_ref, wsd_ref, ln_g_ref, ln_b_ref,
                y_ref, acc_ref, *, alpha, sub):
    e = pl.program_id(1)
    tn = hb_ref.shape[0]

    @pl.when(e == 0)
    def _():
        act = _swiglu(hb_ref[...], wsg_ref[...].astype(BF16), wsu_ref[...].astype(BF16))
        acc_ref[...] = jnp.dot(act.astype(BF16), wsd_ref[...].astype(BF16), preferred_element_type=F32)

    wg = wg_ref[0].astype(BF16)
    wu = wu_ref[0].astype(BF16)
    wd = wd_ref[0].astype(BF16)
    for r0 in range(0, tn, sub):
        gates = g_ref[r0:r0 + sub, :]
        lane = lax.broadcasted_iota(jnp.int32, gates.shape, 1)
        gcol = jnp.sum(jnp.where(lane == e, gates, 0.0), axis=1, keepdims=True)
        act = (_swiglu(hb_ref[r0:r0 + sub, :], wg, wu) * gcol).astype(BF16)
        acc_ref[r0:r0 + sub, :] += jnp.dot(act, wd, preferred_element_type=F32)

    @pl.when(e == pl.num_programs(1) - 1)
    def _():
        y_ref[...] = _layer_norm(alpha * h_ref[...] + acc_ref[...], ln_g_ref[...], ln_b_ref[...])


def _moe(hb, h, gates, wg, wu, wd, wsg, wsu, wsd, ln_g, ln_b, alpha, tn):
    n, d = h.shape
    ne = wg.shape[0]
    row = lambda w: pl.BlockSpec((tn, w), lambda i, e: (i, 0))
    full = lambda a: pl.BlockSpec(a.shape, lambda i, e: (0,) * a.ndim)
    per_expert = lambda a: pl.BlockSpec((1,) + a.shape[1:], lambda i, e: (e, 0, 0))
    return pl.pallas_call(
        functools.partial(_moe_kernel, alpha=alpha, sub=min(tn, MOE_SUB_ROWS)),
        grid=(n // tn, ne),
        in_specs=[row(d), row(d), row(ne), per_expert(wg), per_expert(wu), per_expert(wd),
                  full(wsg), full(wsu), full(wsd), full(ln_g), full(ln_b)],
        out_specs=row(d),
        out_shape=jax.ShapeDtypeStruct((n, d), F32),
        scratch_shapes=[pltpu.VMEM((tn, d), F32)],
        compiler_params=_cparams(("parallel", "arbitrary")),
        name="moe_ln",
    )(hb, h, gates, wg, wu, wd, wsg, wsu, wsd, ln_g, ln_b)


def _pick_tile(n, pref):
    t = min(n, pref)
    assert n % t == 0
    return t


def _feature_major(a):
    n, tokens = a.shape[:2]
    return jnp.moveaxis(a.reshape(n, tokens, -1), 1, 2)


def _token_major(a, feat_shape):
    n, _, tokens = a.shape
    return jnp.moveaxis(a, 1, 2).reshape((n, tokens) + feat_shape)


def kernel(x_prompt, x_sample, cache_diff_k, cache_diff_v, cache_fox_k, cache_fox_v, cache_fox_logf, page_table,
           w_in, b_forget, lambda_q1, lambda_k1, lambda_q2, lambda_k2, subln_gain, w_o, ln1_g, ln1_b, w_router,
           router_bias, w_exp_gate, w_exp_up, w_exp_down, w_sh_gate, w_sh_up, w_sh_down, ln2_g, ln2_b):
    depth = w_in.shape[0]
    nb, seq, d_model = x_prompt.shape
    dec_b, dec_seq, _ = x_sample.shape
    n_pages = page_table.shape[1]
    page = cache_diff_k.shape[2]
    past_len = n_pages * page
    alpha = (2.0 * depth) ** 0.25
    n_s = dec_b * dec_seq

    tq = _pick_tile(seq, 256)
    tm_p = _pick_tile(seq, 256)
    tabs_p = _rope_tables(seq, 0, seq)
    tabs_s = _rope_tables(n_s, past_len, dec_seq)

    xp = x_prompt.reshape(nb * seq, d_model)
    xs = x_sample.reshape(n_s, d_model)
    outs_p, outs_s = [], []
    for l in range(depth):
        lam_init = 0.8 - 0.6 * math.exp(-0.3 * l)
        w = w_in[l]
        cuts = [0, DIFF_W, 2 * DIFF_W, 3 * DIFF_W, 3 * DIFF_W + FOX_W, 3 * DIFF_W + 2 * FOX_W, 3 * DIFF_W + 3 * FOX_W]
        wqd, wkd, wvd, wqf, wkf, wvf = [w[:, cuts[i]:cuts[i + 1]] for i in range(6)]
        wfl = w[:, cuts[6]:]
        pw = {
            "wt": jnp.concatenate([wqd, wqf, wkd, wkf, wvd, wvf], axis=1).T.astype(BF16),
            "wk": jnp.concatenate([wkd, wkf], axis=1).astype(BF16),
            "wtfl": jnp.pad(wfl.T, ((0, BF16_ROWS - N_FOX_HEADS), (0, 0))).astype(BF16),
            "btfl": b_forget[l][:, None],
        }
        lambdas = [v[l][None, :] for v in (lambda_q1, lambda_k1, lambda_q2, lambda_k2)]
        gain = subln_gain[l]
        gain128 = jnp.tile(gain, LANES // HEAD_DIM)[None, :]
        gain512 = jnp.tile(gain, DIFF_W // HEAD_DIM)[None, :]
        wo_bf = w_o[l].astype(BF16)
        wr_pad = jnp.pad(w_router[l], ((0, 0), (0, LANES - N_EXPERTS)))
        rb = router_bias[l][:, None]
        moe_w = (w_exp_gate[l], w_exp_up[l], w_exp_down[l], w_sh_gate[l], w_sh_up[l], w_sh_down[l])
        g1, b1 = ln1_g[l][None, :], ln1_b[l][None, :]
        g2, b2 = ln2_g[l][None, :], ln2_b[l][None, :]

        def ffn(x2d, od, of, tm, tn):
            h, hb, gt = _merge(x2d, od, of, wo_bf, g1, b1, wr_pad, rb, alpha, tm)
            return _moe(hb, h, gt.T, *moe_w, g2, b2, alpha, tn)

        (qdt, qft, kdt, kft, vdt, vft, kdb, kfb, v1d, v1f, logft) = _project(xp, nb, seq, pw, tabs_p, tm_p, tq)
        ct, terms = _cumsum_t(logft)
        n_pairs = N_FOX_HEADS // 2
        cqt_pairs = ct.reshape(nb, n_pairs, 2, seq)
        kbias_pairs = jnp.transpose(terms.reshape(nb, 3, n_pairs, 2, seq), (0, 2, 4, 3, 1)).reshape(nb, n_pairs, seq, 6)
        kbias_pairs = jnp.pad(kbias_pairs, ((0, 0), (0, 0), (0, 0), (0, LANES - 6))).astype(BF16)
        od, of = _prompt_attention(qdt, kdb.reshape(nb, seq, DIFF_W), v1d, qft, kfb.reshape(nb, seq, FOX_W), v1f,
                                   kbias_pairs, cqt_pairs, lambdas, gain128, lam_init, tq)
        xp_new = ffn(xp, od.reshape(nb * seq, DIFF_W), of.reshape(nb * seq, FOX_W),
                     _pick_tile(nb * seq, 256), _pick_tile(nb * seq, 1024))
        outs_p.append((kdt, vdt, kft, vft, logft))

        (sqdt, sqft, skdt, skft, svdt, svft, _, _, _, _, slogft) = _project(xs, 1, n_s, pw, tabs_s, n_s, n_s)
        qd_rows = sqdt[0].T.reshape(dec_b, dec_seq, 2 * N_DIFF_HEADS, DIFF_D)
        eye_hm = jnp.eye(2 * N_DIFF_HEADS, dtype=BF16)
        qd_bd = (qd_rows[:, :, :, None, :] * eye_hm[None, None, :, :, None]).reshape(
            dec_b, dec_seq * 2 * N_DIFF_HEADS, DIFF_W)
        qf_rows = sqft[0].T.reshape(dec_b, dec_seq, N_FOX_HEADS, HEAD_DIM)
        eye_h = jnp.eye(N_FOX_HEADS, dtype=BF16)
        qf_bd = (qf_rows[:, :, :, None, :] * eye_h[None, None, :, :, None]).reshape(
            dec_b, dec_seq * N_FOX_HEADS, FOX_W)
        caches = [_feature_major(c[l]) for c in (cache_diff_k, cache_diff_v, cache_fox_k, cache_fox_v, cache_fox_logf)]
        self_page = lambda a: jnp.pad(jnp.transpose(a[0].reshape(-1, dec_b, dec_seq), (1, 0, 2)),
                                      ((0, 0), (0, 0), (0, page - dec_seq)))
        slf_t = jnp.tile(self_page(slogft), (1, dec_seq, 1))
        selfs = [self_page(skdt), self_page(svdt), self_page(skft), self_page(svft), slf_t]
        od_s, of_s = _sample_attention(page_table, qd_bd, qf_bd, caches, selfs, lambdas, gain512, lam_init, dec_seq)
        od_s = od_s[:, :dec_seq].reshape(n_s, DIFF_W)
        of_s = of_s[:, :dec_seq].reshape(n_s, FOX_W)
        xs_new = ffn(xs, od_s, of_s, n_s, n_s)
        outs_s.append(tuple(a.reshape(a.shape[1], dec_b, dec_seq) for a in (skdt, svdt, skft, svft, slogft)))
        xp, xs = xp_new, xs_new

    def stack_p(idx, feat_shape):
        return jnp.stack([_token_major(o[idx], feat_shape) for o in outs_p])

    def stack_s(idx, feat_shape):
        return jnp.stack([jnp.transpose(o[idx], (1, 2, 0)).reshape((dec_b, dec_seq) + feat_shape) for o in outs_s])

    shapes = ((N_DIFF_HEADS, 2, DIFF_D), (N_DIFF_HEADS, HEAD_DIM), (N_FOX_HEADS, HEAD_DIM), (N_FOX_HEADS, HEAD_DIM),
              (N_FOX_HEADS,))
    return (xp.reshape(nb, seq, d_model), xs.reshape(dec_b, dec_seq, d_model),
            *[stack_p(i, s) for i, s in enumerate(shapes)], *[stack_s(i, s) for i, s in enumerate(shapes)])
```

```python
import functools
import math

import jax
import jax.numpy as jnp
from jax import lax
from jax.experimental import pallas as pl
from jax.experimental.pallas import tpu as pltpu

F32 = jnp.float32
BF16 = jnp.bfloat16

HEAD_DIM = 64
N_DIFF_HEADS = 8
N_FOX_HEADS = 8
DIFF_D = HEAD_DIM // 2
DIFF_W = N_DIFF_HEADS * HEAD_DIM
FOX_W = N_FOX_HEADS * HEAD_DIM
ROT_DIM = DIFF_D // 4
ROT_HALF = ROT_DIM // 2
ROPE_THETA = 500000.0
N_EXPERTS = 64
N_EXPERT_GROUPS = 8
GROUP_SIZE = N_EXPERTS // N_EXPERT_GROUPS
TOPK_GROUPS = 4
TOP_K = 8
ROUTED_SCALE = 2.5
LN_EPS = 1e-5
RMS_EPS = 1e-5
NEG_INF = -1e30
LOG2E = math.log2(math.e)

LANES = 128
SUBLANES = 8
BF16_ROWS = 16
V_ROWS = HEAD_DIM + BF16_ROWS
VMEM_LIMIT = 56 * 1024 * 1024
MOE_SUB_ROWS = 512
N_SCORE_BUFS = 4

NT_DIMS = (((1,), (1,)), ((), ()))


def _cparams(sem):
    return pltpu.CompilerParams(dimension_semantics=sem, vmem_limit_bytes=VMEM_LIMIT)


def _log_sigmoid(x):
    return jnp.minimum(x, 0.0) - jnp.log1p(jnp.exp(-jnp.abs(x)))


def _layer_norm(y, g, b):
    mu = jnp.mean(y, axis=-1, keepdims=True)
    d = y - mu
    var = jnp.mean(d * d, axis=-1, keepdims=True)
    return d * lax.rsqrt(var + LN_EPS) * g + b


def _split3(a):
    a1 = a.astype(BF16)
    r1 = a - a1.astype(F32)
    a2 = r1.astype(BF16)
    a3 = (r1 - a2.astype(F32)).astype(BF16)
    return a1, a2, a3


def _dot_sel_lhs(sel, b):
    return sum(jnp.dot(sel, t, preferred_element_type=F32) for t in _split3(b))


def _dot_sel_rhs(a, sel):
    return sum(jnp.dot(t, sel, preferred_element_type=F32) for t in _split3(a))


def _lambda_full(lq1_ref, lk1_ref, lq2_ref, lk2_ref, lam_init):
    a = jnp.sum(lq1_ref[...] * lk1_ref[...], axis=1, keepdims=True)
    b = jnp.sum(lq2_ref[...] * lk2_ref[...], axis=1, keepdims=True)
    return jnp.exp(a) - jnp.exp(b) + lam_init


def _rope_table_kernel(invf_lane_ref, invf_sub_ref, c_ref, s1_ref, s2_ref, ct_ref, st_ref, *, base, mod, tm):
    i = pl.program_id(0)
    row = lax.broadcasted_iota(jnp.int32, (tm, LANES), 0) + i * tm
    pos = (base + row % mod).astype(F32)
    ang = pos * invf_lane_ref[...]
    li = lax.broadcasted_iota(jnp.int32, (tm, LANES), 1) % DIFF_D
    c = jnp.cos(ang)
    s = jnp.sin(ang)
    c_ref[...] = jnp.where(li < ROT_DIM, c, 1.0)
    s1_ref[...] = jnp.where(li < ROT_HALF, -s, 0.0)
    s2_ref[...] = jnp.where((li >= ROT_HALF) & (li < ROT_DIM), s, 0.0)
    col = lax.broadcasted_iota(jnp.int32, (SUBLANES, tm), 1) + i * tm
    pos_t = (base + col % mod).astype(F32)
    ang_t = invf_sub_ref[...] * pos_t
    sub = lax.broadcasted_iota(jnp.int32, (SUBLANES, tm), 0)
    ct_ref[...] = jnp.cos(ang_t)
    st_ref[...] = jnp.where(sub < ROT_HALF, -jnp.sin(ang_t), jnp.sin(ang_t))


def _rope_tables(n_pos, base, mod):
    inv4 = jnp.power(ROPE_THETA, -jnp.arange(ROT_HALF, dtype=F32) * 2.0 / ROT_DIM)
    invf_lane = inv4[(jnp.arange(LANES) % DIFF_D) % ROT_HALF][None, :]
    invf_sub = inv4[jnp.arange(SUBLANES) % ROT_HALF][:, None]
    tm = min(n_pos, 1024)
    assert n_pos % tm == 0
    lane_spec = pl.BlockSpec((tm, LANES), lambda i: (i, 0))
    sub_spec = pl.BlockSpec((SUBLANES, tm), lambda i: (0, i))
    return pl.pallas_call(
        functools.partial(_rope_table_kernel, base=base, mod=mod, tm=tm),
        grid=(n_pos // tm,),
        in_specs=[pl.BlockSpec((1, LANES), lambda i: (0, 0)), pl.BlockSpec((SUBLANES, 1), lambda i: (0, 0))],
        out_specs=[lane_spec, lane_spec, lane_spec, sub_spec, sub_spec],
        out_shape=[jax.ShapeDtypeStruct((n_pos, LANES), F32)] * 3 + [jax.ShapeDtypeStruct((SUBLANES, n_pos), F32)] * 2,
        compiler_params=_cparams(("parallel",)),
        name="rope_tables",
    )(invf_lane, invf_sub)


def _rope_sublanes(zt, ct, st):
    rows = []
    for j in range(DIFF_W // DIFF_D):
        x8 = zt[DIFF_D * j:DIFF_D * j + ROT_DIM]
        rows.append(x8 * ct + pltpu.roll(x8, ROT_HALF, 0) * st)
        rows.append(zt[DIFF_D * j + ROT_DIM:DIFF_D * (j + 1)])
    return jnp.concatenate(rows, axis=0)


def _proj_kernel(x_ref, wt_ref, wk_ref, wtfl_ref, btfl_ref, c_ref, s1_ref, s2_ref, ct_ref, st_ref,
                 qdt_ref, qft_ref, kdt_ref, kft_ref, vdt_ref, vft_ref, kdb_ref, kfb_ref, v1d_ref, v1f_ref, logft_ref,
                 *, tm, tk):
    xb = x_ref[...].astype(BF16)
    zt = lax.dot_general(wt_ref[...], xb, NT_DIMS, preferred_element_type=F32)
    ct, st = ct_ref[...], st_ref[...]
    w = DIFF_W
    qdt_ref[0] = (_rope_sublanes(zt[0:w], ct, st) * (DIFF_D ** -0.5 * LOG2E)).astype(BF16)
    qft_ref[0] = (zt[w:2 * w] * (HEAD_DIM ** -0.5 * LOG2E)).astype(BF16)
    kdt_ref[0] = _rope_sublanes(zt[2 * w:3 * w], ct, st)
    kft_ref[0] = zt[3 * w:4 * w]
    vd = zt[4 * w:5 * w]
    vf = zt[5 * w:6 * w]
    vdt_ref[0] = vd
    vft_ref[0] = vf
    ones = jnp.ones((BF16_ROWS, tk), BF16)
    for h in range(N_DIFF_HEADS):
        for t in range(tm // tk):
            v1d_ref[0, h, t, 0:HEAD_DIM, :] = vd[HEAD_DIM * h:HEAD_DIM * (h + 1), t * tk:(t + 1) * tk].astype(BF16)
            v1d_ref[0, h, t, HEAD_DIM:V_ROWS, :] = ones
            v1f_ref[0, h, t, 0:HEAD_DIM, :] = vf[HEAD_DIM * h:HEAD_DIM * (h + 1), t * tk:(t + 1) * tk].astype(BF16)
            v1f_ref[0, h, t, HEAD_DIM:V_ROWS, :] = ones
    flt = lax.dot_general(wtfl_ref[...], xb, NT_DIMS, preferred_element_type=F32)
    logft_ref[0] = _log_sigmoid(flt[:N_FOX_HEADS] + btfl_ref[...])

    zk = jnp.dot(xb, wk_ref[...], preferred_element_type=F32)
    c, s1, s2 = c_ref[...], s1_ref[...], s2_ref[...]
    pieces = []
    for j in range(DIFF_W // LANES):
        xk = zk[:, j * LANES:(j + 1) * LANES]
        pieces.append(xk * c + pltpu.roll(xk, LANES - ROT_HALF, 1) * s1 + pltpu.roll(xk, ROT_HALF, 1) * s2)
    kdb_ref[...] = jnp.concatenate(pieces, axis=1).astype(BF16)
    kfb_ref[...] = zk[:, DIFF_W:].astype(BF16)


def _project(x2d, n_batch, seq, w, tables, tm, tk):
    n, d_model = x2d.shape
    nt = seq // tm
    c, s1, s2, ct, st = tables
    full = lambda a: pl.BlockSpec(a.shape, lambda r: (0,) * a.ndim)
    row512 = pl.BlockSpec((tm, DIFF_W), lambda r: (r, 0))
    tab = pl.BlockSpec((tm, LANES), lambda r: (r % nt, 0))
    tabt = pl.BlockSpec((SUBLANES, tm), lambda r: (0, r % nt))
    ft_spec = pl.BlockSpec((1, DIFF_W, tm), lambda r: (r // nt, 0, r % nt))
    v1_spec = pl.BlockSpec((1, N_DIFF_HEADS, tm // tk, V_ROWS, tk), lambda r: (r // nt, 0, r % nt, 0, 0))
    lt_spec = pl.BlockSpec((1, N_FOX_HEADS, tm), lambda r: (r // nt, 0, r % nt))
    sds = jax.ShapeDtypeStruct
    return pl.pallas_call(
        functools.partial(_proj_kernel, tm=tm, tk=tk),
        grid=(n // tm,),
        in_specs=[pl.BlockSpec((tm, d_model), lambda r: (r, 0)),
                  full(w["wt"]), full(w["wk"]), full(w["wtfl"]), full(w["btfl"]), tab, tab, tab, tabt, tabt],
        out_specs=[ft_spec] * 6 + [row512, row512, v1_spec, v1_spec, lt_spec],
        out_shape=[sds((n_batch, DIFF_W, seq), BF16)] * 2 + [sds((n_batch, DIFF_W, seq), F32)] * 4
                  + [sds((n, DIFF_W), BF16)] * 2
                  + [sds((n_batch, N_DIFF_HEADS, seq // tk, V_ROWS, tk), BF16)] * 2
                  + [sds((n_batch, N_FOX_HEADS, seq), F32)],
        compiler_params=_cparams(("parallel",)),
        name="in_proj",
    )(x2d, w["wt"], w["wk"], w["wtfl"], w["btfl"], c, s1, s2, ct, st)


def _lane_cumsum(x):
    lane = lax.broadcasted_iota(jnp.int32, x.shape, 1)
    s = 1
    while s < LANES:
        x = x + jnp.where(lane >= s, pltpu.roll(x, s, 1), 0.0)
        s *= 2
    return x


def _lane_rev_cumsum(x):
    lane = lax.broadcasted_iota(jnp.int32, x.shape, 1)
    s = 1
    while s < LANES:
        x = x + jnp.where(lane < LANES - s, pltpu.roll(x, LANES - s, 1), 0.0)
        s *= 2
    return x


def _cumsum_kernel(lft_ref, ct_ref, terms_ref, *, seq):
    carry = jnp.zeros((N_FOX_HEADS, 1), F32)
    for t in range(seq // LANES):
        x = _lane_cumsum(lft_ref[0, :, t * LANES:(t + 1) * LANES]) + carry
        ct_ref[0, :, t * LANES:(t + 1) * LANES] = x
        for n, term in enumerate(_split3(-LOG2E * x)):
            terms_ref[0, n, :, t * LANES:(t + 1) * LANES] = term.astype(F32)
        carry = x[:, LANES - 1:LANES]


def _cumsum_t(logft):
    nb, nh, seq = logft.shape
    spec = pl.BlockSpec((1, nh, seq), lambda b: (b, 0, 0))
    return pl.pallas_call(
        functools.partial(_cumsum_kernel, seq=seq),
        grid=(nb,), in_specs=[spec],
        out_specs=[spec, pl.BlockSpec((1, 3, nh, seq), lambda b: (b, 0, 0, 0))],
        out_shape=[jax.ShapeDtypeStruct(logft.shape, F32), jax.ShapeDtypeStruct((nb, 3, nh, seq), F32)],
        compiler_params=_cparams(("parallel",)),
        name="logf_cumsum",
    )(logft)


def _block_diag_q(qt, nblk):
    rows_per = LANES // nblk
    rb = lax.broadcasted_iota(jnp.int32, qt.shape, 0) // rows_per
    zero = jnp.zeros_like(qt)
    return jnp.concatenate([jnp.where(rb == j, qt, zero) for j in range(nblk)], axis=1)


def _scores(j, k_ref, qbd_ref, s_ref, tk, kbias_ref=None):
    start = pl.multiple_of(j * tk, tk)
    k = k_ref[0, pl.ds(start, tk), :]
    if kbias_ref is not None:
        k = jnp.concatenate([k, kbias_ref[0, 0, pl.ds(start, tk), :]], axis=1)
    s_ref[...] = jnp.dot(k, qbd_ref[...], preferred_element_type=F32)


def _softmax_pv(j, s_ref, vt_ref, acc_ref, m_ref, *, tq, nblk, blk_head, masked, cq=None):
    tk = tq
    if masked:
        keep = (lax.broadcasted_iota(jnp.int32, (tk, tq), 0) <= lax.broadcasted_iota(jnp.int32, (tk, tq), 1))
    for blk in range(nblk):
        sb = s_ref[:, blk * tq:(blk + 1) * tq]
        if masked:
            sb = jnp.where(keep, sb, NEG_INF)
        m_old = m_ref[blk, 0:1, :]
        smax = jnp.max(sb, axis=0, keepdims=True)
        if cq is not None:
            m_new = jnp.maximum(m_old, smax + cq[blk])
            shift = m_new - cq[blk]
        else:
            m_new = jnp.maximum(m_old, smax)
            shift = m_new
        p = jnp.exp2(sb - shift).astype(BF16)
        alpha = jnp.exp2(m_old - m_new)
        vt = vt_ref[0, blk_head[blk], j]
        acc_ref[blk] = acc_ref[blk] * alpha + jnp.dot(vt, p, preferred_element_type=F32)
        m_ref[blk] = jnp.broadcast_to(m_new, (SUBLANES, tq))


def _causal_sweep(i, scores, update):
    scores(0, 0)

    def body(jj, carry):
        j = N_SCORE_BUFS * jj
        for r in range(N_SCORE_BUFS):
            scores(j + r + 1, (r + 1) % N_SCORE_BUFS)
            update(j + r, r, False)
        return carry

    lax.fori_loop(0, i // N_SCORE_BUFS, body, 0)
    rem = i % N_SCORE_BUFS
    j0 = i - rem
    for r in range(N_SCORE_BUFS - 1):
        @pl.when(r < rem)
        def _():
            scores(j0 + r + 1, r + 1)
            update(j0 + r, r, False)

    for r in range(N_SCORE_BUFS):
        @pl.when(rem == r)
        def _():
            update(i, r, True)


def _attn_init(acc_ref, m_ref):
    acc_ref[...] = jnp.zeros(acc_ref.shape, F32)
    m_ref[...] = jnp.full(m_ref.shape, NEG_INF, F32)


def _attn_kernel(lq1_ref, lk1_ref, lq2_ref, lk2_ref, gain_ref, qdt_ref, kd_ref, v1d_ref, qft_ref, kf_ref, v1f_ref,
                 kbias_ref, cqt_ref, od_ref, of_ref, accd_ref, md_ref, qbdd_ref, accf_ref, mf_ref, qbdf_ref, *s_refs,
                 tq, lam_init):
    i = pl.program_id(2)
    nd, nf = 4, 2
    sd_refs, sf_refs = s_refs[:N_SCORE_BUFS], s_refs[N_SCORE_BUFS:]
    qbdd_ref[...] = _block_diag_q(qdt_ref[0], nd)
    qbdf_ref[0:LANES, :] = _block_diag_q(qft_ref[0], nf)
    r = lax.broadcasted_iota(jnp.int32, (LANES, nf * tq), 0)
    cb = lax.broadcasted_iota(jnp.int32, (LANES, nf * tq), 1) // tq
    qbdf_ref[LANES:2 * LANES, :] = jnp.where((r < 3 * nf) & (r // 3 == cb), 1.0, 0.0).astype(BF16)
    _attn_init(accd_ref, md_ref)
    _attn_init(accf_ref, mf_ref)
    cq = [cqt_ref[0, 0, hh:hh + 1, :] * LOG2E for hh in range(nf)]

    def scores(j, slot):
        _scores(j, kd_ref, qbdd_ref, sd_refs[slot], tq)
        _scores(j, kf_ref, qbdf_ref, sf_refs[slot], tq, kbias_ref)

    def update(j, slot, masked):
        _softmax_pv(j, sd_refs[slot], v1d_ref, accd_ref, md_ref, tq=tq, nblk=nd, blk_head=(0, 0, 1, 1), masked=masked)
        _softmax_pv(j, sf_refs[slot], v1f_ref, accf_ref, mf_ref, tq=tq, nblk=nf, blk_head=(0, 1), masked=masked,
                    cq=cq)

    _causal_sweep(i, scores, update)

    lam = _lambda_full(lq1_ref, lk1_ref, lq2_ref, lk2_ref, lam_init)
    outs = []
    for hh in range(2):
        a1 = accd_ref[2 * hh]
        a2 = accd_ref[2 * hh + 1]
        o = a1[0:HEAD_DIM] / a1[HEAD_DIM:HEAD_DIM + 1] - lam * (a2[0:HEAD_DIM] / a2[HEAD_DIM:HEAD_DIM + 1])
        ms = jnp.mean(o * o, axis=0, keepdims=True)
        outs.append(o * lax.rsqrt(ms + RMS_EPS))
    o = jnp.concatenate(outs, axis=0).T
    od_ref[0] = (o * gain_ref[...] * (1.0 - lam_init)).astype(od_ref.dtype)
    outs = []
    for hh in range(nf):
        a = accf_ref[hh]
        outs.append(a[0:HEAD_DIM] / a[HEAD_DIM:HEAD_DIM + 1])
    of_ref[0] = jnp.concatenate(outs, axis=0).T.astype(of_ref.dtype)


def _prompt_attention(qdt, kdb, v1d, qft, kfb, v1f, kbias_pairs, cqt_pairs, lambdas, gain128, lam_init, tq):
    nb, _, seq = qdt.shape
    n_pairs = DIFF_W // LANES
    qt_spec = pl.BlockSpec((1, LANES, tq), lambda b, p, i: (b, p, i))
    k_spec = pl.BlockSpec((1, seq, LANES), lambda b, p, i: (b, 0, p))
    vt_spec = pl.BlockSpec((1, 2, seq // tq, V_ROWS, tq), lambda b, p, i: (b, p, 0, 0, 0))
    kbias_spec = pl.BlockSpec((1, 1, seq, LANES), lambda b, p, i: (b, p, 0, 0))
    cq_spec = pl.BlockSpec((1, 1, 2, tq), lambda b, p, i: (b, p, 0, i))
    o_spec = pl.BlockSpec((1, tq, LANES), lambda b, p, i: (b, i, p))
    small = lambda a: pl.BlockSpec(a.shape, lambda b, p, i: (0,) * a.ndim)

    def state(nblk, k_width):
        return [pltpu.VMEM((nblk, V_ROWS, tq), F32), pltpu.VMEM((nblk, SUBLANES, tq), F32),
                pltpu.VMEM((k_width, nblk * tq), BF16)]

    score_bufs = lambda nblk: [pltpu.VMEM((tq, nblk * tq), F32)] * N_SCORE_BUFS
    return pl.pallas_call(
        functools.partial(_attn_kernel, tq=tq, lam_init=lam_init),
        grid=(nb, n_pairs, seq // tq),
        in_specs=[small(lambdas[0])] * 4 + [small(gain128), qt_spec, k_spec, vt_spec, qt_spec, k_spec, vt_spec,
                                            kbias_spec, cq_spec],
        out_specs=[o_spec, o_spec],
        out_shape=[jax.ShapeDtypeStruct((nb, seq, DIFF_W), BF16), jax.ShapeDtypeStruct((nb, seq, FOX_W), BF16)],
        scratch_shapes=state(4, LANES) + state(2, 2 * LANES) + score_bufs(4) + score_bufs(2),
        compiler_params=_cparams(("parallel", "parallel", "arbitrary")),
        name="prompt_attention",
    )(*lambdas, gain128, qdt, kdb, v1d, qft, kfb, v1f, kbias_pairs, cqt_pairs)


def _row_softmax_update(s, vts, m_ref, l_ref, acc_ref):
    m_old = m_ref[:, 0:1]
    m_new = jnp.maximum(m_old, jnp.max(s, axis=1, keepdims=True))
    p = jnp.exp2(s - m_new)
    alpha = jnp.exp2(m_old - m_new)
    l_new = alpha * l_ref[:, 0:1] + jnp.sum(p, axis=1, keepdims=True)
    pb = p.astype(BF16)
    pv = sum(lax.dot_general(pb[:, i * LANES:(i + 1) * LANES], vt.astype(BF16), NT_DIMS, preferred_element_type=F32)
             for i, vt in enumerate(vts))
    acc_ref[...] = alpha * acc_ref[...] + pv
    m_ref[...] = jnp.broadcast_to(m_new, m_ref.shape)
    l_ref[...] = jnp.broadcast_to(l_new, l_ref.shape)


def _sample_attn_kernel(pt_ref, lq1_ref, lk1_ref, lq2_ref, lk2_ref, gain_ref, qd_ref, qf_ref, *rest,
                        dec_seq, lam_init, pages_per_step):
    pp = pages_per_step
    dk_refs, dv_refs, fk_refs, fv_refs, lf_refs = (rest[i * pp:(i + 1) * pp] for i in range(5))
    (sdk_ref, sdv_ref, sfk_ref, sfv_ref, slf_ref, od_ref, of_ref,
     md_ref, ld_ref, accd_ref, mf_ref, lfs_ref, accf_ref, carry_ref, cself_ref) = rest[5 * pp:]
    j = pl.program_id(1)
    rd = dec_seq * 2 * N_DIFF_HEADS
    rf = dec_seq * N_FOX_HEADS
    lane_d = lax.broadcasted_iota(jnp.int32, (rd, LANES), 1)
    t_d = lax.broadcasted_iota(jnp.int32, (rd, LANES), 0) // (2 * N_DIFF_HEADS)
    lane_f = lax.broadcasted_iota(jnp.int32, (rf, LANES), 1)
    t_f = lax.broadcasted_iota(jnp.int32, (rf, LANES), 0) // N_FOX_HEADS

    def scores(q_ref, kts):
        q = q_ref[0]
        return jnp.concatenate([jnp.dot(q, kt.astype(BF16), preferred_element_type=F32) for kt in kts], axis=1)

    @pl.when(j == 0)
    def _():
        md_ref[...] = jnp.full(md_ref.shape, NEG_INF, F32)
        ld_ref[...] = jnp.zeros(ld_ref.shape, F32)
        accd_ref[...] = jnp.zeros(accd_ref.shape, F32)
        mf_ref[...] = jnp.full(mf_ref.shape, NEG_INF, F32)
        lfs_ref[...] = jnp.zeros(lfs_ref.shape, F32)
        accf_ref[...] = jnp.zeros(accf_ref.shape, F32)
        carry_ref[...] = jnp.zeros(carry_ref.shape, F32)
        sd = jnp.where((lane_d < dec_seq) & (lane_d <= t_d), scores(qd_ref, [sdk_ref[0]]), NEG_INF)
        _row_softmax_update(sd, [sdv_ref[0]], md_ref, ld_ref, accd_ref)
        incl = _lane_cumsum(slf_ref[0])
        cs_t = jnp.sum(jnp.where(lane_f == t_f, incl, 0.0), axis=1, keepdims=True)
        cself_ref[...] = jnp.broadcast_to(cs_t, cself_ref.shape)
        sf = scores(qf_ref, [sfk_ref[0]]) + (cs_t - incl) * LOG2E
        sf = jnp.where((lane_f < dec_seq) & (lane_f <= t_f), sf, NEG_INF)
        _row_softmax_update(sf, [sfv_ref[0]], mf_ref, lfs_ref, accf_ref)

    _row_softmax_update(scores(qd_ref, [r[0] for r in dk_refs]), [r[0] for r in dv_refs], md_ref, ld_ref, accd_ref)
    cself = cself_ref[:, 0:1]
    carry = carry_ref[:, 0:1]
    biases = []
    for r in lf_refs:
        lf = jnp.concatenate([r[0]] * dec_seq, axis=0)
        suffix = _lane_rev_cumsum(lf)
        biases.append(cself + carry + (suffix - lf))
        carry = carry + suffix[:, 0:1]
    carry_ref[...] = jnp.broadcast_to(carry, carry_ref.shape)
    sf = scores(qf_ref, [r[0] for r in fk_refs]) + jnp.concatenate(biases, axis=1) * LOG2E
    _row_softmax_update(sf, [r[0] for r in fv_refs], mf_ref, lfs_ref, accf_ref)

    @pl.when(j == pl.num_programs(1) - 1)
    def _():
        lam = _lambda_full(lq1_ref, lk1_ref, lq2_ref, lk2_ref, lam_init)
        width = accd_ref.shape[1]
        accn = accd_ref[...] / ld_ref[:, 0:1]
        col_h = lax.broadcasted_iota(jnp.int32, (rd, width), 1) // HEAD_DIM
        row = lax.broadcasted_iota(jnp.int32, (rd, width), 0)
        accn = jnp.where(col_h == (row % (2 * N_DIFF_HEADS)) // 2, accn, 0.0)
        sr = lax.broadcasted_iota(jnp.int32, (BF16_ROWS, rd), 0)
        sc = lax.broadcasted_iota(jnp.int32, (BF16_ROWS, rd), 1)
        o_maps = []
        for mp in range(2):
            sel = ((sc // (2 * N_DIFF_HEADS) == sr) & (sc % 2 == mp)).astype(BF16)
            o_maps.append(_dot_sel_lhs(sel, accn))
        od = o_maps[0] - lam * o_maps[1]
        gi = lax.broadcasted_iota(jnp.int32, (width, width), 0) // HEAD_DIM
        gj = lax.broadcasted_iota(jnp.int32, (width, width), 1) // HEAD_DIM
        gmat = jnp.where(gi == gj, 1.0 / HEAD_DIM, 0.0).astype(BF16)
        ms = _dot_sel_rhs(od * od, gmat)
        od_ref[0] = (od * lax.rsqrt(ms + RMS_EPS) * gain_ref[...] * (1.0 - lam_init)).astype(od_ref.dtype)

        accfn = accf_ref[...] / lfs_ref[:, 0:1]
        col_hf = lax.broadcasted_iota(jnp.int32, (rf, width), 1) // HEAD_DIM
        row_f = lax.broadcasted_iota(jnp.int32, (rf, width), 0)
        accfn = jnp.where(col_hf == row_f % N_FOX_HEADS, accfn, 0.0)
        srf = lax.broadcasted_iota(jnp.int32, (BF16_ROWS, rf), 0)
        scf = lax.broadcasted_iota(jnp.int32, (BF16_ROWS, rf), 1)
        self_ = (scf // N_FOX_HEADS == srf).astype(BF16)
        of_ref[0] = _dot_sel_lhs(self_, accfn).astype(of_ref.dtype)


def _sample_attention(page_table, qd_bd, qf_bd, caches, selfs, lambdas, gain512, lam_init, dec_seq):
    nb, n_pages = page_table.shape
    page = caches[0].shape[2]
    assert page == LANES
    rd, rf = qd_bd.shape[1], qf_bd.shape[1]
    pp = next(c for c in (8, 4, 2, 1) if n_pages % c == 0)
    n_steps = n_pages // pp

    def page_map(slot):
        return lambda b, j, pt: (pt[b, n_pages - 1 - (j * pp + slot)], 0, 0)

    per_b = lambda b, j, pt: (b, 0, 0)
    small = lambda a: pl.BlockSpec(a.shape, lambda b, j, pt: (0,) * a.ndim)
    b_spec = lambda a: pl.BlockSpec((1,) + a.shape[1:], per_b)
    out_spec = pl.BlockSpec((1, BF16_ROWS, DIFF_W), per_b)
    cache_specs, cache_args = [], []
    for a in caches:
        for slot in range(pp):
            cache_specs.append(pl.BlockSpec((1,) + a.shape[1:], page_map(slot)))
            cache_args.append(a)
    grid_spec = pltpu.PrefetchScalarGridSpec(
        num_scalar_prefetch=1,
        grid=(nb, n_steps),
        in_specs=[small(lambdas[0])] * 4 + [small(gain512), b_spec(qd_bd), b_spec(qf_bd)]
                 + cache_specs + [b_spec(a) for a in selfs],
        out_specs=[out_spec, out_spec],
        scratch_shapes=[pltpu.VMEM((rd, LANES), F32), pltpu.VMEM((rd, LANES), F32), pltpu.VMEM((rd, DIFF_W), F32),
                        pltpu.VMEM((rf, LANES), F32), pltpu.VMEM((rf, LANES), F32), pltpu.VMEM((rf, FOX_W), F32),
                        pltpu.VMEM((rf, LANES), F32), pltpu.VMEM((rf, LANES), F32)],
    )
    return pl.pallas_call(
        functools.partial(_sample_attn_kernel, dec_seq=dec_seq, lam_init=lam_init, pages_per_step=pp),
        grid_spec=grid_spec,
        out_shape=[jax.ShapeDtypeStruct((nb, BF16_ROWS, DIFF_W), BF16)] * 2,
        compiler_params=_cparams(("parallel", "arbitrary")),
        name="sample_attention",
    )(page_table, *lambdas, gain512, qd_bd, qf_bd, *cache_args, *selfs)


def _merge_kernel(x_ref, od_ref, of_ref, wo_ref, g_ref, b_ref, wr_ref, rb_ref, h_ref, hb_ref, gt_ref, *, alpha):
    o = jnp.concatenate([od_ref[...], of_ref[...]], axis=1)
    mix = jnp.dot(o, wo_ref[...], preferred_element_type=F32)
    h = _layer_norm(alpha * x_ref[...] + mix, g_ref[...], b_ref[...])
    h_ref[...] = h
    hb_ref[...] = h.astype(BF16)

    tm = h.shape[0]
    h1, h2, _ = _split3(h)
    w1, w2, _ = _split3(wr_ref[...])
    mm = lambda a, b: jnp.dot(a, b, preferred_element_type=F32)
    logits = (mm(h1, w1) + (mm(h1, w2) + mm(h2, w1))).T[:N_EXPERTS]
    scores = jax.nn.sigmoid(logits)
    choice = scores + rb_ref[...]
    member = lax.broadcasted_iota(jnp.int32, (GROUP_SIZE, tm), 0)
    blocks, gscore = [], []
    for g in range(N_EXPERT_GROUPS):
        blk = choice[g * GROUP_SIZE:(g + 1) * GROUP_SIZE]
        m1 = jnp.max(blk, axis=0, keepdims=True)
        first = jnp.min(jnp.where(blk == m1, member, GROUP_SIZE), axis=0, keepdims=True)
        m2 = jnp.max(jnp.where(member == first, -jnp.inf, blk), axis=0, keepdims=True)
        blocks.append(blk)
        gscore.append(m1 + m2)
    masked = []
    for g in range(N_EXPERT_GROUPS):
        rank = jnp.zeros((1, tm), jnp.int32)
        for g2 in range(N_EXPERT_GROUPS):
            if g2 == g:
                continue
            beats = (gscore[g2] > gscore[g]) | ((gscore[g2] == gscore[g]) & (g2 < g))
            rank = rank + beats.astype(jnp.int32)
        masked.append(jnp.where(rank < TOPK_GROUPS, blocks[g], NEG_INF))
    vm = jnp.concatenate(masked, axis=0)
    eidx = lax.broadcasted_iota(jnp.int32, (N_EXPERTS, tm), 0)
    cnt = jnp.zeros((N_EXPERTS, tm), jnp.int32)
    for e2 in range(N_EXPERTS):
        r = vm[e2:e2 + 1]
        beats = (r > vm) | ((r == vm) & (e2 < eidx))
        cnt = cnt + beats.astype(jnp.int32)
    w = jnp.where(cnt < TOP_K, scores, 0.0)
    gt_ref[...] = w / (jnp.sum(w, axis=0, keepdims=True) + 1e-20) * ROUTED_SCALE


def _merge(x2d, od, of, wo_bf, g, b, wr_pad, rb, alpha, tm):
    n, d = x2d.shape
    row = lambda w: pl.BlockSpec((tm, w), lambda r: (r, 0))
    full = lambda a: pl.BlockSpec(a.shape, lambda r: (0,) * a.ndim)
    return pl.pallas_call(
        functools.partial(_merge_kernel, alpha=alpha),
        grid=(n // tm,),
        in_specs=[row(d), row(DIFF_W), row(FOX_W), full(wo_bf), full(g), full(b), full(wr_pad), full(rb)],
        out_specs=[row(d), row(d), pl.BlockSpec((N_EXPERTS, tm), lambda r: (0, r))],
        out_shape=[jax.ShapeDtypeStruct((n, d), F32), jax.ShapeDtypeStruct((n, d), BF16),
                   jax.ShapeDtypeStruct((N_EXPERTS, n), F32)],
        compiler_params=_cparams(("parallel",)),
        name="merge_ln_router",
    )(x2d, od, of, wo_bf, g, b, wr_pad, rb)


def _swiglu(hb, wg, wu):
    a = jnp.dot(hb, wg, preferred_element_type=F32)
    u = jnp.dot(hb, wu, preferred_element_type=F32)
    return jax.nn.silu(a) * u


def _moe_kernel(hb_ref, h_ref, g_ref, wg_ref, wu_ref, wd_ref, wsg_ref, wsu_ref, wsd_ref, ln_g_ref, ln_b_ref,
                y_ref, acc_ref, *, alpha, sub):
    e = pl.program_id(1)
    tn = hb_ref.shape[0]

    @pl.when(e == 0)
    def _():
        act = _swiglu(hb_ref[...], wsg_ref[...].astype(BF16), wsu_ref[...].astype(BF16))
        acc_ref[...] = jnp.dot(act.astype(BF16), wsd_ref[...].astype(BF16), preferred_element_type=F32)

    wg = wg_ref[0].astype(BF16)
    wu = wu_ref[0].astype(BF16)
    wd = wd_ref[0].astype(BF16)
    for r0 in range(0, tn, sub):
        gates = g_ref[r0:r0 + sub, :]
        lane = lax.broadcasted_iota(jnp.int32, gates.shape, 1)
        gcol = jnp.sum(jnp.where(lane == e, gates, 0.0), axis=1, keepdims=True)
        act = (_swiglu(hb_ref[r0:r0 + sub, :], wg, wu) * gcol).astype(BF16)
        acc_ref[r0:r0 + sub, :] += jnp.dot(act, wd, preferred_element_type=F32)

    @pl.when(e == pl.num_programs(1) - 1)
    def _():
        y_ref[...] = _layer_norm(alpha * h_ref[...] + acc_ref[...], ln_g_ref[...], ln_b_ref[...])


def _moe(hb, h, gates, wg, wu, wd, wsg, wsu, wsd, ln_g, ln_b, alpha, tn):
    n, d = h.shape
    ne = wg.shape[0]
    row = lambda w: pl.BlockSpec((tn, w), lambda i, e: (i, 0))
    full = lambda a: pl.BlockSpec(a.shape, lambda i, e: (0,) * a.ndim)
    per_expert = lambda a: pl.BlockSpec((1,) + a.shape[1:], lambda i, e: (e, 0, 0))
    return pl.pallas_call(
        functools.partial(_moe_kernel, alpha=alpha, sub=min(tn, MOE_SUB_ROWS)),
        grid=(n // tn, ne),
        in_specs=[row(d), row(d), row(ne), per_expert(wg), per_expert(wu), per_expert(wd),
                  full(wsg), full(wsu), full(wsd), full(ln_g), full(ln_b)],
        out_specs=row(d),
        out_shape=jax.ShapeDtypeStruct((n, d), F32),
        scratch_shapes=[pltpu.VMEM((tn, d), F32)],
        compiler_params=_cparams(("parallel", "arbitrary")),
        name="moe_ln",
    )(hb, h, gates, wg, wu, wd, wsg, wsu, wsd, ln_g, ln_b)


def _pick_tile(n, pref):
    t = min(n, pref)
    assert n % t == 0
    return t


def _feature_major(a):
    n, tokens = a.shape[:2]
    return jnp.moveaxis(a.reshape(n, tokens, -1), 1, 2)


def _token_major(a, feat_shape):
    n, _, tokens = a.shape
    return jnp.moveaxis(a, 1, 2).reshape((n, tokens) + feat_shape)


def kernel(x_prompt, x_sample, cache_diff_k, cache_diff_v, cache_fox_k, cache_fox_v, cache_fox_logf, page_table,
           w_in, b_forget, lambda_q1, lambda_k1, lambda_q2, lambda_k2, subln_gain, w_o, ln1_g, ln1_b, w_router,
           router_bias, w_exp_gate, w_exp_up, w_exp_down, w_sh_gate, w_sh_up, w_sh_down, ln2_g, ln2_b):
    depth = w_in.shape[0]
    nb, seq, d_model = x_prompt.shape
    dec_b, dec_seq, _ = x_sample.shape
    n_pages = page_table.shape[1]
    page = cache_diff_k.shape[2]
    past_len = n_pages * page
    alpha = (2.0 * depth) ** 0.25
    n_s = dec_b * dec_seq

    tq = _pick_tile(seq, 256)
    tm_p = _pick_tile(seq, 256)
    tabs_p = _rope_tables(seq, 0, seq)
    tabs_s = _rope_tables(n_s, past_len, dec_seq)

    xp = x_prompt.reshape(nb * seq, d_model)
    xs = x_sample.reshape(n_s, d_model)
    outs_p, outs_s = [], []
    for l in range(depth):
        lam_init = 0.8 - 0.6 * math.exp(-0.3 * l)
        w = w_in[l]
        cuts = [0, DIFF_W, 2 * DIFF_W, 3 * DIFF_W, 3 * DIFF_W + FOX_W, 3 * DIFF_W + 2 * FOX_W, 3 * DIFF_W + 3 * FOX_W]
        wqd, wkd, wvd, wqf, wkf, wvf = [w[:, cuts[i]:cuts[i + 1]] for i in range(6)]
        wfl = w[:, cuts[6]:]
        pw = {
            "wt": jnp.concatenate([wqd, wqf, wkd, wkf, wvd, wvf], axis=1).T.astype(BF16),
            "wk": jnp.concatenate([wkd, wkf], axis=1).astype(BF16),
            "wtfl": jnp.pad(wfl.T, ((0, BF16_ROWS - N_FOX_HEADS), (0, 0))).astype(BF16),
            "btfl": b_forget[l][:, None],
        }
        lambdas = [v[l][None, :] for v in (lambda_q1, lambda_k1, lambda_q2, lambda_k2)]
        gain = subln_gain[l]
        gain128 = jnp.tile(gain, LANES // HEAD_DIM)[None, :]
        gain512 = jnp.tile(gain, DIFF_W // HEAD_DIM)[None, :]
        wo_bf = w_o[l].astype(BF16)
        wr_pad = jnp.pad(w_router[l], ((0, 0), (0, LANES - N_EXPERTS)))
        rb = router_bias[l][:, None]
        moe_w = (w_exp_gate[l], w_exp_up[l], w_exp_down[l], w_sh_gate[l], w_sh_up[l], w_sh_down[l])
        g1, b1 = ln1_g[l][None, :], ln1_b[l][None, :]
        g2, b2 = ln2_g[l][None, :], ln2_b[l][None, :]

        def ffn(x2d, od, of, tm, tn):
            h, hb, gt = _merge(x2d, od, of, wo_bf, g1, b1, wr_pad, rb, alpha, tm)
            return _moe(hb, h, gt.T, *moe_w, g2, b2, alpha, tn)

        (qdt, qft, kdt, kft, vdt, vft, kdb, kfb, v1d, v1f, logft) = _project(xp, nb, seq, pw, tabs_p, tm_p, tq)
        ct, terms = _cumsum_t(logft)
        n_pairs = N_FOX_HEADS // 2
        cqt_pairs = ct.reshape(nb, n_pairs, 2, seq)
        kbias_pairs = jnp.transpose(terms.reshape(nb, 3, n_pairs, 2, seq), (0, 2, 4, 3, 1)).reshape(nb, n_pairs, seq, 6)
        kbias_pairs = jnp.pad(kbias_pairs, ((0, 0), (0, 0), (0, 0), (0, LANES - 6))).astype(BF16)
        od, of = _prompt_attention(qdt, kdb.reshape(nb, seq, DIFF_W), v1d, qft, kfb.reshape(nb, seq, FOX_W), v1f,
                                   kbias_pairs, cqt_pairs, lambdas, gain128, lam_init, tq)
        xp_new = ffn(xp, od.reshape(nb * seq, DIFF_W), of.reshape(nb * seq, FOX_W),
                     _pick_tile(nb * seq, 256), _pick_tile(nb * seq, 1024))
        outs_p.append((kdt, vdt, kft, vft, logft))

        (sqdt, sqft, skdt, skft, svdt, svft, _, _, _, _, slogft) = _project(xs, 1, n_s, pw, tabs_s, n_s, n_s)
        qd_rows = sqdt[0].T.reshape(dec_b, dec_seq, 2 * N_DIFF_HEADS, DIFF_D)
        eye_hm = jnp.eye(2 * N_DIFF_HEADS, dtype=BF16)
        qd_bd = (qd_rows[:, :, :, None, :] * eye_hm[None, None, :, :, None]).reshape(
            dec_b, dec_seq * 2 * N_DIFF_HEADS, DIFF_W)
        qf_rows = sqft[0].T.reshape(dec_b, dec_seq, N_FOX_HEADS, HEAD_DIM)
        eye_h = jnp.eye(N_FOX_HEADS, dtype=BF16)
        qf_bd = (qf_rows[:, :, :, None, :] * eye_h[None, None, :, :, None]).reshape(
            dec_b, dec_seq * N_FOX_HEADS, FOX_W)
        caches = [_feature_major(c[l]) for c in (cache_diff_k, cache_diff_v, cache_fox_k, cache_fox_v, cache_fox_logf)]
        self_page = lambda a: jnp.pad(jnp.transpose(a[0].reshape(-1, dec_b, dec_seq), (1, 0, 2)),
                                      ((0, 0), (0, 0), (0, page - dec_seq)))
        slf_t = jnp.tile(self_page(slogft), (1, dec_seq, 1))
        selfs = [self_page(skdt), self_page(svdt), self_page(skft), self_page(svft), slf_t]
        od_s, of_s = _sample_attention(page_table, qd_bd, qf_bd, caches, selfs, lambdas, gain512, lam_init, dec_seq)
        od_s = od_s[:, :dec_seq].reshape(n_s, DIFF_W)
        of_s = of_s[:, :dec_seq].reshape(n_s, FOX_W)
        xs_new = ffn(xs, od_s, of_s, n_s, n_s)
        outs_s.append(tuple(a.reshape(a.shape[1], dec_b, dec_seq) for a in (skdt, svdt, skft, svft, slogft)))
        xp, xs = xp_new, xs_new

    def stack_p(idx, feat_shape):
        return jnp.stack([_token_major(o[idx], feat_shape) for o in outs_p])

    def stack_s(idx, feat_shape):
        return jnp.stack([jnp.transpose(o[idx], (1, 2, 0)).reshape((dec_b, dec_seq) + feat_shape) for o in outs_s])

    shapes = ((N_DIFF_HEADS, 2, DIFF_D), (N_DIFF_HEADS, HEAD_DIM), (N_FOX_HEADS, HEAD_DIM), (N_FOX_HEADS, HEAD_DIM),
              (N_FOX_HEADS,))
    return (xp.reshape(nb, seq, d_model), xs.reshape(dec_b, dec_seq, d_model),
            *[stack_p(i, s) for i, s in enumerate(shapes)], *[stack_s(i, s) for i, s in enumerate(shapes)])
```

```python
import functools
import math

import jax
import jax.numpy as jnp
from jax import lax
from jax.experimental import pallas as pl
from jax.experimental.pallas import tpu as pltpu

F32 = jnp.float32
BF16 = jnp.bfloat16

HEAD_DIM = 64
N_DIFF_HEADS = 8
N_FOX_HEADS = 8
DIFF_D = HEAD_DIM // 2
DIFF_W = N_DIFF_HEADS * HEAD_DIM
FOX_W = N_FOX_HEADS * HEAD_DIM
ROT_DIM = DIFF_D // 4
ROT_HALF = ROT_DIM // 2
ROPE_THETA = 500000.0
N_EXPERTS = 64
N_EXPERT_GROUPS = 8
GROUP_SIZE = N_EXPERTS // N_EXPERT_GROUPS
TOPK_GROUPS = 4
TOP_K = 8
ROUTED_SCALE = 2.5
LN_EPS = 1e-5
RMS_EPS = 1e-5
NEG_INF = -1e30
LOG2E = math.log2(math.e)

LANES = 128
SUBLANES = 8
BF16_ROWS = 16
V_ROWS = HEAD_DIM + BF16_ROWS
VMEM_LIMIT = 56 * 1024 * 1024
MOE_SUB_ROWS = 512
MOE_EXPERTS_PER_STEP = 2
N_SCORE_BUFS = 4

NT_DIMS = (((1,), (1,)), ((), ()))


def _cparams(sem):
    return pltpu.CompilerParams(dimension_semantics=sem, vmem_limit_bytes=VMEM_LIMIT)


def _log_sigmoid(x):
    return jnp.minimum(x, 0.0) - jnp.log1p(jnp.exp(-jnp.abs(x)))


def _layer_norm(y, g, b):
    mu = jnp.mean(y, axis=-1, keepdims=True)
    d = y - mu
    var = jnp.mean(d * d, axis=-1, keepdims=True)
    return d * lax.rsqrt(var + LN_EPS) * g + b


def _split3(a):
    a1 = a.astype(BF16)
    r1 = a - a1.astype(F32)
    a2 = r1.astype(BF16)
    a3 = (r1 - a2.astype(F32)).astype(BF16)
    return a1, a2, a3


def _dot_sel_lhs(sel, b):
    return sum(jnp.dot(sel, t, preferred_element_type=F32) for t in _split3(b))


def _dot_sel_rhs(a, sel):
    return sum(jnp.dot(t, sel, preferred_element_type=F32) for t in _split3(a))


def _lambda_full(lq1_ref, lk1_ref, lq2_ref, lk2_ref, lam_init):
    a = jnp.sum(lq1_ref[...] * lk1_ref[...], axis=1, keepdims=True)
    b = jnp.sum(lq2_ref[...] * lk2_ref[...], axis=1, keepdims=True)
    return jnp.exp(a) - jnp.exp(b) + lam_init


def _rope_table_kernel(invf_sub_ref, ct_ref, st_ref, *, base, mod, tm):
    i = pl.program_id(0)
    col = lax.broadcasted_iota(jnp.int32, (SUBLANES, tm), 1) + i * tm
    pos_t = (base + col % mod).astype(F32)
    ang_t = invf_sub_ref[...] * pos_t
    sub = lax.broadcasted_iota(jnp.int32, (SUBLANES, tm), 0)
    ct_ref[...] = jnp.cos(ang_t)
    st_ref[...] = jnp.where(sub < ROT_HALF, -jnp.sin(ang_t), jnp.sin(ang_t))


def _rope_tables(n_pos, base, mod):
    inv4 = jnp.power(ROPE_THETA, -jnp.arange(ROT_HALF, dtype=F32) * 2.0 / ROT_DIM)
    invf_sub = inv4[jnp.arange(SUBLANES) % ROT_HALF][:, None]
    tm = min(n_pos, 1024)
    assert n_pos % tm == 0
    sub_spec = pl.BlockSpec((SUBLANES, tm), lambda i: (0, i))
    return pl.pallas_call(
        functools.partial(_rope_table_kernel, base=base, mod=mod, tm=tm),
        grid=(n_pos // tm,),
        in_specs=[pl.BlockSpec((SUBLANES, 1), lambda i: (0, 0))],
        out_specs=[sub_spec, sub_spec],
        out_shape=[jax.ShapeDtypeStruct((SUBLANES, n_pos), F32)] * 2,
        compiler_params=_cparams(("parallel",)),
        name="rope_tables",
    )(invf_sub)


def _rope_sublanes(zt, ct, st):
    rows = []
    for j in range(DIFF_W // DIFF_D):
        x8 = zt[DIFF_D * j:DIFF_D * j + ROT_DIM]
        rows.append(x8 * ct + pltpu.roll(x8, ROT_HALF, 0) * st)
        rows.append(zt[DIFF_D * j + ROT_DIM:DIFF_D * (j + 1)])
    return jnp.concatenate(rows, axis=0)


def _proj_kernel(x_ref, wt_ref, wtfl_ref, btfl_ref, ct_ref, st_ref,
                 qdt_ref, qft_ref, kdt_ref, kft_ref, vdt_ref, vft_ref, kdb_ref, kfb_ref, v1d_ref, v1f_ref, logft_ref,
                 *, tm, tk):
    xb = x_ref[...].astype(BF16)
    zt = lax.dot_general(wt_ref[...], xb, NT_DIMS, preferred_element_type=F32)
    ct, st = ct_ref[...], st_ref[...]
    w = DIFF_W
    qdt_ref[0] = (_rope_sublanes(zt[0:w], ct, st) * (DIFF_D ** -0.5 * LOG2E)).astype(BF16)
    qft_ref[0] = (zt[w:2 * w] * (HEAD_DIM ** -0.5 * LOG2E)).astype(BF16)
    kd = _rope_sublanes(zt[2 * w:3 * w], ct, st)
    kf = zt[3 * w:4 * w]
    kdt_ref[0] = kd
    kft_ref[0] = kf
    kdb_ref[...] = kd.T.astype(BF16)
    kfb_ref[...] = kf.T.astype(BF16)
    vd = zt[4 * w:5 * w]
    vf = zt[5 * w:6 * w]
    vdt_ref[0] = vd
    vft_ref[0] = vf
    ones = jnp.ones((BF16_ROWS, tk), BF16)
    for h in range(N_DIFF_HEADS):
        for t in range(tm // tk):
            v1d_ref[0, h, t, 0:HEAD_DIM, :] = vd[HEAD_DIM * h:HEAD_DIM * (h + 1), t * tk:(t + 1) * tk].astype(BF16)
            v1d_ref[0, h, t, HEAD_DIM:V_ROWS, :] = ones
            v1f_ref[0, h, t, 0:HEAD_DIM, :] = vf[HEAD_DIM * h:HEAD_DIM * (h + 1), t * tk:(t + 1) * tk].astype(BF16)
            v1f_ref[0, h, t, HEAD_DIM:V_ROWS, :] = ones
    flt = lax.dot_general(wtfl_ref[...], xb, NT_DIMS, preferred_element_type=F32)
    logft_ref[0] = _log_sigmoid(flt[:N_FOX_HEADS] + btfl_ref[...])


def _project(x2d, n_batch, seq, w, tables, tm, tk):
    n, d_model = x2d.shape
    nt = seq // tm
    ct, st = tables
    full = lambda a: pl.BlockSpec(a.shape, lambda r: (0,) * a.ndim)
    row512 = pl.BlockSpec((tm, DIFF_W), lambda r: (r, 0))
    tabt = pl.BlockSpec((SUBLANES, tm), lambda r: (0, r % nt))
    ft_spec = pl.BlockSpec((1, DIFF_W, tm), lambda r: (r // nt, 0, r % nt))
    v1_spec = pl.BlockSpec((1, N_DIFF_HEADS, tm // tk, V_ROWS, tk), lambda r: (r // nt, 0, r % nt, 0, 0))
    lt_spec = pl.BlockSpec((1, N_FOX_HEADS, tm), lambda r: (r // nt, 0, r % nt))
    sds = jax.ShapeDtypeStruct
    return pl.pallas_call(
        functools.partial(_proj_kernel, tm=tm, tk=tk),
        grid=(n // tm,),
        in_specs=[pl.BlockSpec((tm, d_model), lambda r: (r, 0)),
                  full(w["wt"]), full(w["wtfl"]), full(w["btfl"]), tabt, tabt],
        out_specs=[ft_spec] * 6 + [row512, row512, v1_spec, v1_spec, lt_spec],
        out_shape=[sds((n_batch, DIFF_W, seq), BF16)] * 2 + [sds((n_batch, DIFF_W, seq), F32)] * 4
                  + [sds((n, DIFF_W), BF16)] * 2
                  + [sds((n_batch, N_DIFF_HEADS, seq // tk, V_ROWS, tk), BF16)] * 2
                  + [sds((n_batch, N_FOX_HEADS, seq), F32)],
        compiler_params=_cparams(("parallel",)),
        name="in_proj",
    )(x2d, w["wt"], w["wtfl"], w["btfl"], ct, st)


def _lane_cumsum(x):
    lane = lax.broadcasted_iota(jnp.int32, x.shape, 1)
    s = 1
    while s < LANES:
        x = x + jnp.where(lane >= s, pltpu.roll(x, s, 1), 0.0)
        s *= 2
    return x


def _lane_rev_cumsum(x):
    lane = lax.broadcasted_iota(jnp.int32, x.shape, 1)
    s = 1
    while s < LANES:
        x = x + jnp.where(lane < LANES - s, pltpu.roll(x, LANES - s, 1), 0.0)
        s *= 2
    return x


def _cumsum_kernel(lft_ref, ct_ref, terms_ref, *, seq):
    carry = jnp.zeros((N_FOX_HEADS, 1), F32)
    for t in range(seq // LANES):
        x = _lane_cumsum(lft_ref[0, :, t * LANES:(t + 1) * LANES]) + carry
        ct_ref[0, :, t * LANES:(t + 1) * LANES] = x
        for n, term in enumerate(_split3(-LOG2E * x)):
            terms_ref[0, n, :, t * LANES:(t + 1) * LANES] = term.astype(F32)
        carry = x[:, LANES - 1:LANES]


def _cumsum_t(logft):
    nb, nh, seq = logft.shape
    spec = pl.BlockSpec((1, nh, seq), lambda b: (b, 0, 0))
    return pl.pallas_call(
        functools.partial(_cumsum_kernel, seq=seq),
        grid=(nb,), in_specs=[spec],
        out_specs=[spec, pl.BlockSpec((1, 3, nh, seq), lambda b: (b, 0, 0, 0))],
        out_shape=[jax.ShapeDtypeStruct(logft.shape, F32), jax.ShapeDtypeStruct((nb, 3, nh, seq), F32)],
        compiler_params=_cparams(("parallel",)),
        name="logf_cumsum",
    )(logft)


def _block_diag_q(qt, nblk):
    rows_per = LANES // nblk
    rb = lax.broadcasted_iota(jnp.int32, qt.shape, 0) // rows_per
    zero = jnp.zeros_like(qt)
    return jnp.concatenate([jnp.where(rb == j, qt, zero) for j in range(nblk)], axis=1)


def _scores(j, k_ref, qbd_ref, s_ref, tk, kbias_ref=None):
    start = pl.multiple_of(j * tk, tk)
    k = k_ref[0, pl.ds(start, tk), :]
    if kbias_ref is not None:
        k = jnp.concatenate([k, kbias_ref[0, 0, pl.ds(start, tk), :]], axis=1)
    s_ref[...] = jnp.dot(k, qbd_ref[...], preferred_element_type=F32)


def _softmax_pv(j, s_ref, vt_ref, acc_ref, m_ref, *, tq, nblk, blk_head, masked, cq=None):
    tk = tq
    if masked:
        keep = (lax.broadcasted_iota(jnp.int32, (tk, tq), 0) <= lax.broadcasted_iota(jnp.int32, (tk, tq), 1))
    for blk in range(nblk):
        sb = s_ref[:, blk * tq:(blk + 1) * tq]
        if masked:
            sb = jnp.where(keep, sb, NEG_INF)
        m_old = m_ref[blk, 0:1, :]
        smax = jnp.max(sb, axis=0, keepdims=True)
        if cq is not None:
            m_new = jnp.maximum(m_old, smax + cq[blk])
            shift = m_new - cq[blk]
        else:
            m_new = jnp.maximum(m_old, smax)
            shift = m_new
        p = jnp.exp2(sb - shift).astype(BF16)
        alpha = jnp.exp2(m_old - m_new)
        vt = vt_ref[0, blk_head[blk], j]
        acc_ref[blk] = acc_ref[blk] * alpha + jnp.dot(vt, p, preferred_element_type=F32)
        m_ref[blk] = jnp.broadcast_to(m_new, (SUBLANES, tq))


def _causal_sweep(i, scores, update):
    scores(0, 0)

    def body(jj, carry):
        j = N_SCORE_BUFS * jj
        for r in range(N_SCORE_BUFS):
            scores(j + r + 1, (r + 1) % N_SCORE_BUFS)
            update(j + r, r, False)
        return carry

    lax.fori_loop(0, i // N_SCORE_BUFS, body, 0)
    rem = i % N_SCORE_BUFS
    j0 = i - rem
    for r in range(N_SCORE_BUFS - 1):
        @pl.when(r < rem)
        def _():
            scores(j0 + r + 1, r + 1)
            update(j0 + r, r, False)

    for r in range(N_SCORE_BUFS):
        @pl.when(rem == r)
        def _():
            update(i, r, True)


def _attn_init(acc_ref, m_ref):
    acc_ref[...] = jnp.zeros(acc_ref.shape, F32)
    m_ref[...] = jnp.full(m_ref.shape, NEG_INF, F32)


def _attn_kernel(lq1_ref, lk1_ref, lq2_ref, lk2_ref, gain_ref, qdt_ref, kd_ref, v1d_ref, qft_ref, kf_ref, v1f_ref,
                 kbias_ref, cqt_ref, od_ref, of_ref, accd_ref, md_ref, qbdd_ref, accf_ref, mf_ref, qbdf_ref, *s_refs,
                 tq, lam_init):
    i = pl.program_id(2)
    nd, nf = 4, 2
    sd_refs, sf_refs = s_refs[:N_SCORE_BUFS], s_refs[N_SCORE_BUFS:]
    qbdd_ref[...] = _block_diag_q(qdt_ref[0], nd)
    qbdf_ref[0:LANES, :] = _block_diag_q(qft_ref[0], nf)
    r = lax.broadcasted_iota(jnp.int32, (LANES, nf * tq), 0)
    cb = lax.broadcasted_iota(jnp.int32, (LANES, nf * tq), 1) // tq
    qbdf_ref[LANES:2 * LANES, :] = jnp.where((r < 3 * nf) & (r // 3 == cb), 1.0, 0.0).astype(BF16)
    _attn_init(accd_ref, md_ref)
    _attn_init(accf_ref, mf_ref)
    cq = [cqt_ref[0, 0, hh:hh + 1, :] * LOG2E for hh in range(nf)]

    def scores(j, slot):
        _scores(j, kd_ref, qbdd_ref, sd_refs[slot], tq)
        _scores(j, kf_ref, qbdf_ref, sf_refs[slot], tq, kbias_ref)

    def update(j, slot, masked):
        _softmax_pv(j, sd_refs[slot], v1d_ref, accd_ref, md_ref, tq=tq, nblk=nd, blk_head=(0, 0, 1, 1), masked=masked)
        _softmax_pv(j, sf_refs[slot], v1f_ref, accf_ref, mf_ref, tq=tq, nblk=nf, blk_head=(0, 1), masked=masked,
                    cq=cq)

    _causal_sweep(i, scores, update)

    lam = _lambda_full(lq1_ref, lk1_ref, lq2_ref, lk2_ref, lam_init)
    outs = []
    for hh in range(2):
        a1 = accd_ref[2 * hh]
        a2 = accd_ref[2 * hh + 1]
        o = a1[0:HEAD_DIM] / a1[HEAD_DIM:HEAD_DIM + 1] - lam * (a2[0:HEAD_DIM] / a2[HEAD_DIM:HEAD_DIM + 1])
        ms = jnp.mean(o * o, axis=0, keepdims=True)
        outs.append(o * lax.rsqrt(ms + RMS_EPS))
    o = jnp.concatenate(outs, axis=0).T
    od_ref[0] = (o * gain_ref[...] * (1.0 - lam_init)).astype(od_ref.dtype)
    outs = []
    for hh in range(nf):
        a = accf_ref[hh]
        outs.append(a[0:HEAD_DIM] / a[HEAD_DIM:HEAD_DIM + 1])
    of_ref[0] = jnp.concatenate(outs, axis=0).T.astype(of_ref.dtype)


def _prompt_attention(qdt, kdb, v1d, qft, kfb, v1f, kbias_pairs, cqt_pairs, lambdas, gain128, lam_init, tq):
    nb, _, seq = qdt.shape
    n_pairs = DIFF_W // LANES
    qt_spec = pl.BlockSpec((1, LANES, tq), lambda b, p, i: (b, p, i))
    k_spec = pl.BlockSpec((1, seq, LANES), lambda b, p, i: (b, 0, p))
    vt_spec = pl.BlockSpec((1, 2, seq // tq, V_ROWS, tq), lambda b, p, i: (b, p, 0, 0, 0))
    kbias_spec = pl.BlockSpec((1, 1, seq, LANES), lambda b, p, i: (b, p, 0, 0))
    cq_spec = pl.BlockSpec((1, 1, 2, tq), lambda b, p, i: (b, p, 0, i))
    o_spec = pl.BlockSpec((1, tq, LANES), lambda b, p, i: (b, i, p))
    small = lambda a: pl.BlockSpec(a.shape, lambda b, p, i: (0,) * a.ndim)

    def state(nblk, k_width):
        return [pltpu.VMEM((nblk, V_ROWS, tq), F32), pltpu.VMEM((nblk, SUBLANES, tq), F32),
                pltpu.VMEM((k_width, nblk * tq), BF16)]

    score_bufs = lambda nblk: [pltpu.VMEM((tq, nblk * tq), F32)] * N_SCORE_BUFS
    return pl.pallas_call(
        functools.partial(_attn_kernel, tq=tq, lam_init=lam_init),
        grid=(nb, n_pairs, seq // tq),
        in_specs=[small(lambdas[0])] * 4 + [small(gain128), qt_spec, k_spec, vt_spec, qt_spec, k_spec, vt_spec,
                                            kbias_spec, cq_spec],
        out_specs=[o_spec, o_spec],
        out_shape=[jax.ShapeDtypeStruct((nb, seq, DIFF_W), BF16), jax.ShapeDtypeStruct((nb, seq, FOX_W), BF16)],
        scratch_shapes=state(4, LANES) + state(2, 2 * LANES) + score_bufs(4) + score_bufs(2),
        compiler_params=_cparams(("parallel", "parallel", "arbitrary")),
        name="prompt_attention",
    )(*lambdas, gain128, qdt, kdb, v1d, qft, kfb, v1f, kbias_pairs, cqt_pairs)


def _row_softmax_update(s, vts, m_ref, l_ref, acc_ref):
    m_old = m_ref[:, 0:1]
    m_new = jnp.maximum(m_old, jnp.max(s, axis=1, keepdims=True))
    p = jnp.exp2(s - m_new)
    alpha = jnp.exp2(m_old - m_new)
    l_new = alpha * l_ref[:, 0:1] + jnp.sum(p, axis=1, keepdims=True)
    pb = p.astype(BF16)
    pv = sum(lax.dot_general(pb[:, i * LANES:(i + 1) * LANES], vt.astype(BF16), NT_DIMS, preferred_element_type=F32)
             for i, vt in enumerate(vts))
    acc_ref[...] = alpha * acc_ref[...] + pv
    m_ref[...] = jnp.broadcast_to(m_new, m_ref.shape)
    l_ref[...] = jnp.broadcast_to(l_new, l_ref.shape)


def _sample_attn_kernel(pt_ref, lq1_ref, lk1_ref, lq2_ref, lk2_ref, gain_ref, qd_ref, qf_ref, *rest,
                        dec_seq, lam_init, pages_per_step):
    pp = pages_per_step
    dk_refs, dv_refs, fk_refs, fv_refs, lf_refs = (rest[i * pp:(i + 1) * pp] for i in range(5))
    (sdk_ref, sdv_ref, sfk_ref, sfv_ref, slf_ref, od_ref, of_ref,
     md_ref, ld_ref, accd_ref, mf_ref, lfs_ref, accf_ref, carry_ref, cself_ref) = rest[5 * pp:]
    j = pl.program_id(1)
    rd = dec_seq * 2 * N_DIFF_HEADS
    rf = dec_seq * N_FOX_HEADS
    lane_d = lax.broadcasted_iota(jnp.int32, (rd, LANES), 1)
    t_d = lax.broadcasted_iota(jnp.int32, (rd, LANES), 0) // (2 * N_DIFF_HEADS)
    lane_f = lax.broadcasted_iota(jnp.int32, (rf, LANES), 1)
    t_f = lax.broadcasted_iota(jnp.int32, (rf, LANES), 0) // N_FOX_HEADS

    def scores(q_ref, kts):
        q = q_ref[0]
        return jnp.concatenate([jnp.dot(q, kt.astype(BF16), preferred_element_type=F32) for kt in kts], axis=1)

    @pl.when(j == 0)
    def _():
        md_ref[...] = jnp.full(md_ref.shape, NEG_INF, F32)
        ld_ref[...] = jnp.zeros(ld_ref.shape, F32)
        accd_ref[...] = jnp.zeros(accd_ref.shape, F32)
        mf_ref[...] = jnp.full(mf_ref.shape, NEG_INF, F32)
        lfs_ref[...] = jnp.zeros(lfs_ref.shape, F32)
        accf_ref[...] = jnp.zeros(accf_ref.shape, F32)
        carry_ref[...] = jnp.zeros(carry_ref.shape, F32)
        sd = jnp.where((lane_d < dec_seq) & (lane_d <= t_d), scores(qd_ref, [sdk_ref[0]]), NEG_INF)
        _row_softmax_update(sd, [sdv_ref[0]], md_ref, ld_ref, accd_ref)
        incl = _lane_cumsum(slf_ref[0])
        cs_t = jnp.sum(jnp.where(lane_f == t_f, incl, 0.0), axis=1, keepdims=True)
        cself_ref[...] = jnp.broadcast_to(cs_t, cself_ref.shape)
        sf = scores(qf_ref, [sfk_ref[0]]) + (cs_t - incl) * LOG2E
        sf = jnp.where((lane_f < dec_seq) & (lane_f <= t_f), sf, NEG_INF)
        _row_softmax_update(sf, [sfv_ref[0]], mf_ref, lfs_ref, accf_ref)

    _row_softmax_update(scores(qd_ref, [r[0] for r in dk_refs]), [r[0] for r in dv_refs], md_ref, ld_ref, accd_ref)
    cself = cself_ref[:, 0:1]
    carry = carry_ref[:, 0:1]
    biases = []
    for r in lf_refs:
        lf = jnp.concatenate([r[0]] * dec_seq, axis=0)
        suffix = _lane_rev_cumsum(lf)
        biases.append(cself + carry + (suffix - lf))
        carry = carry + suffix[:, 0:1]
    carry_ref[...] = jnp.broadcast_to(carry, carry_ref.shape)
    sf = scores(qf_ref, [r[0] for r in fk_refs]) + jnp.concatenate(biases, axis=1) * LOG2E
    _row_softmax_update(sf, [r[0] for r in fv_refs], mf_ref, lfs_ref, accf_ref)

    @pl.when(j == pl.num_programs(1) - 1)
    def _():
        lam = _lambda_full(lq1_ref, lk1_ref, lq2_ref, lk2_ref, lam_init)
        width = accd_ref.shape[1]
        accn = accd_ref[...] / ld_ref[:, 0:1]
        col_h = lax.broadcasted_iota(jnp.int32, (rd, width), 1) // HEAD_DIM
        row = lax.broadcasted_iota(jnp.int32, (rd, width), 0)
        accn = jnp.where(col_h == (row % (2 * N_DIFF_HEADS)) // 2, accn, 0.0)
        sr = lax.broadcasted_iota(jnp.int32, (BF16_ROWS, rd), 0)
        sc = lax.broadcasted_iota(jnp.int32, (BF16_ROWS, rd), 1)
        o_maps = []
        for mp in range(2):
            sel = ((sc // (2 * N_DIFF_HEADS) == sr) & (sc % 2 == mp)).astype(BF16)
            o_maps.append(_dot_sel_lhs(sel, accn))
        od = o_maps[0] - lam * o_maps[1]
        gi = lax.broadcasted_iota(jnp.int32, (width, width), 0) // HEAD_DIM
        gj = lax.broadcasted_iota(jnp.int32, (width, width), 1) // HEAD_DIM
        gmat = jnp.where(gi == gj, 1.0 / HEAD_DIM, 0.0).astype(BF16)
        ms = _dot_sel_rhs(od * od, gmat)
        od_ref[0] = (od * lax.rsqrt(ms + RMS_EPS) * gain_ref[...] * (1.0 - lam_init)).astype(od_ref.dtype)

        accfn = accf_ref[...] / lfs_ref[:, 0:1]
        col_hf = lax.broadcasted_iota(jnp.int32, (rf, width), 1) // HEAD_DIM
        row_f = lax.broadcasted_iota(jnp.int32, (rf, width), 0)
        accfn = jnp.where(col_hf == row_f % N_FOX_HEADS, accfn, 0.0)
        srf = lax.broadcasted_iota(jnp.int32, (BF16_ROWS, rf), 0)
        scf = lax.broadcasted_iota(jnp.int32, (BF16_ROWS, rf), 1)
        self_ = (scf // N_FOX_HEADS == srf).astype(BF16)
        of_ref[0] = _dot_sel_lhs(self_, accfn).astype(of_ref.dtype)


def _sample_attention(page_table, qd_bd, qf_bd, caches, selfs, lambdas, gain512, lam_init, dec_seq):
    nb, n_pages = page_table.shape
    page = caches[0].shape[2]
    assert page == LANES
    rd, rf = qd_bd.shape[1], qf_bd.shape[1]
    pp = next(c for c in (8, 4, 2, 1) if n_pages % c == 0)
    n_steps = n_pages // pp

    def page_map(slot):
        return lambda b, j, pt: (pt[b, n_pages - 1 - (j * pp + slot)], 0, 0)

    per_b = lambda b, j, pt: (b, 0, 0)
    small = lambda a: pl.BlockSpec(a.shape, lambda b, j, pt: (0,) * a.ndim)
    b_spec = lambda a: pl.BlockSpec((1,) + a.shape[1:], per_b)
    out_spec = pl.BlockSpec((1, BF16_ROWS, DIFF_W), per_b)
    cache_specs, cache_args = [], []
    for a in caches:
        for slot in range(pp):
            cache_specs.append(pl.BlockSpec((1,) + a.shape[1:], page_map(slot)))
            cache_args.append(a)
    grid_spec = pltpu.PrefetchScalarGridSpec(
        num_scalar_prefetch=1,
        grid=(nb, n_steps),
        in_specs=[small(lambdas[0])] * 4 + [small(gain512), b_spec(qd_bd), b_spec(qf_bd)]
                 + cache_specs + [b_spec(a) for a in selfs],
        out_specs=[out_spec, out_spec],
        scratch_shapes=[pltpu.VMEM((rd, LANES), F32), pltpu.VMEM((rd, LANES), F32), pltpu.VMEM((rd, DIFF_W), F32),
                        pltpu.VMEM((rf, LANES), F32), pltpu.VMEM((rf, LANES), F32), pltpu.VMEM((rf, FOX_W), F32),
                        pltpu.VMEM((rf, LANES), F32), pltpu.VMEM((rf, LANES), F32)],
    )
    return pl.pallas_call(
        functools.partial(_sample_attn_kernel, dec_seq=dec_seq, lam_init=lam_init, pages_per_step=pp),
        grid_spec=grid_spec,
        out_shape=[jax.ShapeDtypeStruct((nb, BF16_ROWS, DIFF_W), BF16)] * 2,
        compiler_params=_cparams(("parallel", "arbitrary")),
        name="sample_attention",
    )(page_table, *lambdas, gain512, qd_bd, qf_bd, *cache_args, *selfs)


def _merge_kernel(x_ref, od_ref, of_ref, wo_ref, g_ref, b_ref, wr_ref, rb_ref, h_ref, hb_ref, gt_ref, *, alpha):
    o = jnp.concatenate([od_ref[...], of_ref[...]], axis=1)
    mix = jnp.dot(o, wo_ref[...], preferred_element_type=F32)
    h = _layer_norm(alpha * x_ref[...] + mix, g_ref[...], b_ref[...])
    h_ref[...] = h
    hb_ref[...] = h.astype(BF16)

    tm = h.shape[0]
    h1, h2, _ = _split3(h)
    w1, w2, _ = _split3(wr_ref[...])
    mm = lambda a, b: jnp.dot(a, b, preferred_element_type=F32)
    logits = (mm(h1, w1) + (mm(h1, w2) + mm(h2, w1))).T[:N_EXPERTS]
    scores = jax.nn.sigmoid(logits)
    choice = scores + rb_ref[...]
    member = lax.broadcasted_iota(jnp.int32, (GROUP_SIZE, tm), 0)
    blocks, gscore = [], []
    for g in range(N_EXPERT_GROUPS):
        blk = choice[g * GROUP_SIZE:(g + 1) * GROUP_SIZE]
        m1 = jnp.max(blk, axis=0, keepdims=True)
        first = jnp.min(jnp.where(blk == m1, member, GROUP_SIZE), axis=0, keepdims=True)
        m2 = jnp.max(jnp.where(member == first, -jnp.inf, blk), axis=0, keepdims=True)
        blocks.append(blk)
        gscore.append(m1 + m2)
    masked = []
    for g in range(N_EXPERT_GROUPS):
        rank = jnp.zeros((1, tm), jnp.int32)
        for g2 in range(N_EXPERT_GROUPS):
            if g2 == g:
                continue
            beats = (gscore[g2] > gscore[g]) | ((gscore[g2] == gscore[g]) & (g2 < g))
            rank = rank + beats.astype(jnp.int32)
        masked.append(jnp.where(rank < TOPK_GROUPS, blocks[g], NEG_INF))
    vm = jnp.concatenate(masked, axis=0)
    eidx = lax.broadcasted_iota(jnp.int32, (N_EXPERTS, tm), 0)
    cnt = jnp.zeros((N_EXPERTS, tm), jnp.int32)
    for e2 in range(N_EXPERTS):
        r = vm[e2:e2 + 1]
        beats = (r > vm) | ((r == vm) & (e2 < eidx))
        cnt = cnt + beats.astype(jnp.int32)
    w = jnp.where(cnt < TOP_K, scores, 0.0)
    gt_ref[...] = w / (jnp.sum(w, axis=0, keepdims=True) + 1e-20) * ROUTED_SCALE


def _merge(x2d, od, of, wo_bf, g, b, wr_pad, rb, alpha, tm):
    n, d = x2d.shape
    row = lambda w: pl.BlockSpec((tm, w), lambda r: (r, 0))
    full = lambda a: pl.BlockSpec(a.shape, lambda r: (0,) * a.ndim)
    return pl.pallas_call(
        functools.partial(_merge_kernel, alpha=alpha),
        grid=(n // tm,),
        in_specs=[row(d), row(DIFF_W), row(FOX_W), full(wo_bf), full(g), full(b), full(wr_pad), full(rb)],
        out_specs=[row(d), row(d), pl.BlockSpec((N_EXPERTS, tm), lambda r: (0, r))],
        out_shape=[jax.ShapeDtypeStruct((n, d), F32), jax.ShapeDtypeStruct((n, d), BF16),
                   jax.ShapeDtypeStruct((N_EXPERTS, n), F32)],
        compiler_params=_cparams(("parallel",)),
        name="merge_ln_router",
    )(x2d, od, of, wo_bf, g, b, wr_pad, rb)


def _swiglu(hb, wg, wu):
    a = jnp.dot(hb, wg, preferred_element_type=F32)
    u = jnp.dot(hb, wu, preferred_element_type=F32)
    return jax.nn.silu(a) * u


def _moe_kernel(hb_ref, h_ref, g_ref, wg_ref, wu_ref, wd_ref, wsg_ref, wsu_ref, wsd_ref, ln_g_ref, ln_b_ref,
                y_ref, acc_ref, *, alpha, sub):
    step = pl.program_id(1)
    tn = hb_ref.shape[0]
    per_step = wg_ref.shape[0]

    @pl.when(step == 0)
    def _():
        act = _swiglu(hb_ref[...], wsg_ref[...].astype(BF16), wsu_ref[...].astype(BF16))
        acc_ref[...] = jnp.dot(act.astype(BF16), wsd_ref[...].astype(BF16), preferred_element_type=F32)

    weights = [(wg_ref[k].astype(BF16), wu_ref[k].astype(BF16), wd_ref[k].astype(BF16)) for k in range(per_step)]
    for r0 in range(0, tn, sub):
        gates = g_ref[r0:r0 + sub, :]
        lane = lax.broadcasted_iota(jnp.int32, gates.shape, 1)
        hb = hb_ref[r0:r0 + sub, :]
        contrib = None
        for k, (wg, wu, wd) in enumerate(weights):
            gcol = jnp.sum(jnp.where(lane == step * per_step + k, gates, 0.0), axis=1, keepdims=True)
            act = (_swiglu(hb, wg, wu) * gcol).astype(BF16)
            part = jnp.dot(act, wd, preferred_element_type=F32)
            contrib = part if contrib is None else contrib + part
        acc_ref[r0:r0 + sub, :] += contrib

    @pl.when(step == pl.num_programs(1) - 1)
    def _():
        y_ref[...] = _layer_norm(alpha * h_ref[...] + acc_ref[...], ln_g_ref[...], ln_b_ref[...])


def _moe(hb, h, gates, wg, wu, wd, wsg, wsu, wsd, ln_g, ln_b, alpha, tn):
    n, d = h.shape
    ne = wg.shape[0]
    row = lambda w: pl.BlockSpec((tn, w), lambda i, e: (i, 0))
    full = lambda a: pl.BlockSpec(a.shape, lambda i, e: (0,) * a.ndim)
    assert ne % MOE_EXPERTS_PER_STEP == 0
    per_expert = lambda a: pl.BlockSpec((MOE_EXPERTS_PER_STEP,) + a.shape[1:], lambda i, e: (e, 0, 0))
    return pl.pallas_call(
        functools.partial(_moe_kernel, alpha=alpha, sub=min(tn, MOE_SUB_ROWS)),
        grid=(n // tn, ne // MOE_EXPERTS_PER_STEP),
        in_specs=[row(d), row(d), row(ne), per_expert(wg), per_expert(wu), per_expert(wd),
                  full(wsg), full(wsu), full(wsd), full(ln_g), full(ln_b)],
        out_specs=row(d),
        out_shape=jax.ShapeDtypeStruct((n, d), F32),
        scratch_shapes=[pltpu.VMEM((tn, d), F32)],
        compiler_params=_cparams(("parallel", "arbitrary")),
        name="moe_ln",
    )(hb, h, gates, wg, wu, wd, wsg, wsu, wsd, ln_g, ln_b)


def _pick_tile(n, pref):
    t = min(n, pref)
    assert n % t == 0
    return t


def _feature_major(a):
    n, tokens = a.shape[:2]
    return jnp.moveaxis(a.reshape(n, tokens, -1), 1, 2)


def _token_major(a, feat_shape):
    n, _, tokens = a.shape
    return jnp.moveaxis(a, 1, 2).reshape((n, tokens) + feat_shape)


def kernel(x_prompt, x_sample, cache_diff_k, cache_diff_v, cache_fox_k, cache_fox_v, cache_fox_logf, page_table,
           w_in, b_forget, lambda_q1, lambda_k1, lambda_q2, lambda_k2, subln_gain, w_o, ln1_g, ln1_b, w_router,
           router_bias, w_exp_gate, w_exp_up, w_exp_down, w_sh_gate, w_sh_up, w_sh_down, ln2_g, ln2_b):
    depth = w_in.shape[0]
    nb, seq, d_model = x_prompt.shape
    dec_b, dec_seq, _ = x_sample.shape
    n_pages = page_table.shape[1]
    page = cache_diff_k.shape[2]
    past_len = n_pages * page
    alpha = (2.0 * depth) ** 0.25
    n_s = dec_b * dec_seq

    tq = _pick_tile(seq, 256)
    tm_p = _pick_tile(seq, 256)
    tabs_p = _rope_tables(seq, 0, seq)
    tabs_s = _rope_tables(n_s, past_len, dec_seq)

    xp = x_prompt.reshape(nb * seq, d_model)
    xs = x_sample.reshape(n_s, d_model)
    outs_p, outs_s = [], []
    for l in range(depth):
        lam_init = 0.8 - 0.6 * math.exp(-0.3 * l)
        w = w_in[l]
        cuts = [0, DIFF_W, 2 * DIFF_W, 3 * DIFF_W, 3 * DIFF_W + FOX_W, 3 * DIFF_W + 2 * FOX_W, 3 * DIFF_W + 3 * FOX_W]
        wqd, wkd, wvd, wqf, wkf, wvf = [w[:, cuts[i]:cuts[i + 1]] for i in range(6)]
        wfl = w[:, cuts[6]:]
        pw = {
            "wt": jnp.concatenate([wqd, wqf, wkd, wkf, wvd, wvf], axis=1).T.astype(BF16),
            "wtfl": jnp.pad(wfl.T, ((0, BF16_ROWS - N_FOX_HEADS), (0, 0))).astype(BF16),
            "btfl": b_forget[l][:, None],
        }
        lambdas = [v[l][None, :] for v in (lambda_q1, lambda_k1, lambda_q2, lambda_k2)]
        gain = subln_gain[l]
        gain128 = jnp.tile(gain, LANES // HEAD_DIM)[None, :]
        gain512 = jnp.tile(gain, DIFF_W // HEAD_DIM)[None, :]
        wo_bf = w_o[l].astype(BF16)
        wr_pad = jnp.pad(w_router[l], ((0, 0), (0, LANES - N_EXPERTS)))
        rb = router_bias[l][:, None]
        moe_w = (w_exp_gate[l], w_exp_up[l], w_exp_down[l], w_sh_gate[l], w_sh_up[l], w_sh_down[l])
        g1, b1 = ln1_g[l][None, :], ln1_b[l][None, :]
        g2, b2 = ln2_g[l][None, :], ln2_b[l][None, :]

        def ffn(x2d, od, of, tm, tn):
            h, hb, gt = _merge(x2d, od, of, wo_bf, g1, b1, wr_pad, rb, alpha, tm)
            return _moe(hb, h, gt.T, *moe_w, g2, b2, alpha, tn)

        (qdt, qft, kdt, kft, vdt, vft, kdb, kfb, v1d, v1f, logft) = _project(xp, nb, seq, pw, tabs_p, tm_p, tq)
        ct, terms = _cumsum_t(logft)
        n_pairs = N_FOX_HEADS // 2
        cqt_pairs = ct.reshape(nb, n_pairs, 2, seq)
        kbias_pairs = jnp.transpose(terms.reshape(nb, 3, n_pairs, 2, seq), (0, 2, 4, 3, 1)).reshape(nb, n_pairs, seq, 6)
        kbias_pairs = jnp.pad(kbias_pairs, ((0, 0), (0, 0), (0, 0), (0, LANES - 6))).astype(BF16)
        od, of = _prompt_attention(qdt, kdb.reshape(nb, seq, DIFF_W), v1d, qft, kfb.reshape(nb, seq, FOX_W), v1f,
                                   kbias_pairs, cqt_pairs, lambdas, gain128, lam_init, tq)
        xp_new = ffn(xp, od.reshape(nb * seq, DIFF_W), of.reshape(nb * seq, FOX_W),
                     _pick_tile(nb * seq, 256), _pick_tile(nb * seq, 1024))
        outs_p.append((kdt, vdt, kft, vft, logft))

        (sqdt, sqft, skdt, skft, svdt, svft, _, _, _, _, slogft) = _project(xs, 1, n_s, pw, tabs_s, n_s, n_s)
        qd_rows = sqdt[0].T.reshape(dec_b, dec_seq, 2 * N_DIFF_HEADS, DIFF_D)
        eye_hm = jnp.eye(2 * N_DIFF_HEADS, dtype=BF16)
        qd_bd = (qd_rows[:, :, :, None, :] * eye_hm[None, None, :, :, None]).reshape(
            dec_b, dec_seq * 2 * N_DIFF_HEADS, DIFF_W)
        qf_rows = sqft[0].T.reshape(dec_b, dec_seq, N_FOX_HEADS, HEAD_DIM)
        eye_h = jnp.eye(N_FOX_HEADS, dtype=BF16)
        qf_bd = (qf_rows[:, :, :, None, :] * eye_h[None, None, :, :, None]).reshape(
            dec_b, dec_seq * N_FOX_HEADS, FOX_W)
        caches = [_feature_major(c[l]) for c in (cache_diff_k, cache_diff_v, cache_fox_k, cache_fox_v, cache_fox_logf)]
        self_page = lambda a: jnp.pad(jnp.transpose(a[0].reshape(-1, dec_b, dec_seq), (1, 0, 2)),
                                      ((0, 0), (0, 0), (0, page - dec_seq)))
        slf_t = jnp.tile(self_page(slogft), (1, dec_seq, 1))
        selfs = [self_page(skdt), self_page(svdt), self_page(skft), self_page(svft), slf_t]
        od_s, of_s = _sample_attention(page_table, qd_bd, qf_bd, caches, selfs, lambdas, gain512, lam_init, dec_seq)
        od_s = od_s[:, :dec_seq].reshape(n_s, DIFF_W)
        of_s = of_s[:, :dec_seq].reshape(n_s, FOX_W)
        xs_new = ffn(xs, od_s, of_s, n_s, n_s)
        outs_s.append(tuple(a.reshape(a.shape[1], dec_b, dec_seq) for a in (skdt, svdt, skft, svft, slogft)))
        xp, xs = xp_new, xs_new

    def stack_p(idx, feat_shape):
        return jnp.stack([_token_major(o[idx], feat_shape) for o in outs_p])

    def stack_s(idx, feat_shape):
        return jnp.stack([jnp.transpose(o[idx], (1, 2, 0)).reshape((dec_b, dec_seq) + feat_shape) for o in outs_s])

    shapes = ((N_DIFF_HEADS, 2, DIFF_D), (N_DIFF_HEADS, HEAD_DIM), (N_FOX_HEADS, HEAD_DIM), (N_FOX_HEADS, HEAD_DIM),
              (N_FOX_HEADS,))
    return (xp.reshape(nb, seq, d_model), xs.reshape(dec_b, dec_seq, d_model),
            *[stack_p(i, s) for i, s in enumerate(shapes)], *[stack_s(i, s) for i, s in enumerate(shapes)])
```

```python
import functools
import math

import jax
import jax.numpy as jnp
from jax import lax
from jax.experimental import pallas as pl
from jax.experimental.pallas import tpu as pltpu

F32 = jnp.float32
BF16 = jnp.bfloat16

HEAD_DIM = 64
N_DIFF_HEADS = 8
N_FOX_HEADS = 8
DIFF_D = HEAD_DIM // 2
DIFF_W = N_DIFF_HEADS * HEAD_DIM
FOX_W = N_FOX_HEADS * HEAD_DIM
ROT_DIM = DIFF_D // 4
ROT_HALF = ROT_DIM // 2
ROPE_THETA = 500000.0
N_EXPERTS = 64
N_EXPERT_GROUPS = 8
GROUP_SIZE = N_EXPERTS // N_EXPERT_GROUPS
TOPK_GROUPS = 4
TOP_K = 8
ROUTED_SCALE = 2.5
LN_EPS = 1e-5
RMS_EPS = 1e-5
NEG_INF = -1e30
LOG2E = math.log2(math.e)

LANES = 128
SUBLANES = 8
BF16_ROWS = 16
V_ROWS = HEAD_DIM + BF16_ROWS
VMEM_LIMIT = 56 * 1024 * 1024
MOE_SUB_ROWS = 512
MOE_EXPERTS_PER_STEP = 2
N_SCORE_BUFS = 4

NT_DIMS = (((1,), (1,)), ((), ()))


def _cparams(sem):
    return pltpu.CompilerParams(dimension_semantics=sem, vmem_limit_bytes=VMEM_LIMIT)


def _log_sigmoid(x):
    return jnp.minimum(x, 0.0) - jnp.log1p(jnp.exp(-jnp.abs(x)))


def _layer_norm(y, g, b):
    mu = jnp.mean(y, axis=-1, keepdims=True)
    d = y - mu
    var = jnp.mean(d * d, axis=-1, keepdims=True)
    return d * lax.rsqrt(var + LN_EPS) * g + b


def _split3(a):
    a1 = a.astype(BF16)
    r1 = a - a1.astype(F32)
    a2 = r1.astype(BF16)
    a3 = (r1 - a2.astype(F32)).astype(BF16)
    return a1, a2, a3


def _dot_sel_lhs(sel, b):
    return sum(jnp.dot(sel, t, preferred_element_type=F32) for t in _split3(b))


def _dot_sel_rhs(a, sel):
    return sum(jnp.dot(t, sel, preferred_element_type=F32) for t in _split3(a))


def _lambda_full(lq1_ref, lk1_ref, lq2_ref, lk2_ref, lam_init):
    a = jnp.sum(lq1_ref[...] * lk1_ref[...], axis=1, keepdims=True)
    b = jnp.sum(lq2_ref[...] * lk2_ref[...], axis=1, keepdims=True)
    return jnp.exp(a) - jnp.exp(b) + lam_init


def _rope_table_kernel(invf_sub_ref, ct_ref, st_ref, *, base, mod, tm):
    i = pl.program_id(0)
    col = lax.broadcasted_iota(jnp.int32, (SUBLANES, tm), 1) + i * tm
    pos_t = (base + col % mod).astype(F32)
    ang_t = invf_sub_ref[...] * pos_t
    sub = lax.broadcasted_iota(jnp.int32, (SUBLANES, tm), 0)
    ct_ref[...] = jnp.cos(ang_t)
    st_ref[...] = jnp.where(sub < ROT_HALF, -jnp.sin(ang_t), jnp.sin(ang_t))


def _rope_tables(n_pos, base, mod):
    inv4 = jnp.power(ROPE_THETA, -jnp.arange(ROT_HALF, dtype=F32) * 2.0 / ROT_DIM)
    invf_sub = inv4[jnp.arange(SUBLANES) % ROT_HALF][:, None]
    tm = min(n_pos, 1024)
    assert n_pos % tm == 0
    sub_spec = pl.BlockSpec((SUBLANES, tm), lambda i: (0, i))
    return pl.pallas_call(
        functools.partial(_rope_table_kernel, base=base, mod=mod, tm=tm),
        grid=(n_pos // tm,),
        in_specs=[pl.BlockSpec((SUBLANES, 1), lambda i: (0, 0))],
        out_specs=[sub_spec, sub_spec],
        out_shape=[jax.ShapeDtypeStruct((SUBLANES, n_pos), F32)] * 2,
        compiler_params=_cparams(("parallel",)),
        name="rope_tables",
    )(invf_sub)


def _rope_sublanes(zt, ct, st):
    rows = []
    for j in range(DIFF_W // DIFF_D):
        x8 = zt[DIFF_D * j:DIFF_D * j + ROT_DIM]
        rows.append(x8 * ct + pltpu.roll(x8, ROT_HALF, 0) * st)
        rows.append(zt[DIFF_D * j + ROT_DIM:DIFF_D * (j + 1)])
    return jnp.concatenate(rows, axis=0)


def _proj_kernel(x_ref, wt_ref, wtfl_ref, btfl_ref, ct_ref, st_ref,
                 qdt_ref, qft_ref, kdt_ref, kft_ref, vdt_ref, vft_ref, kdb_ref, kfb_ref, v1d_ref, v1f_ref, logft_ref,
                 *, tm, tk):
    xb = x_ref[...].astype(BF16)
    zt = lax.dot_general(wt_ref[...], xb, NT_DIMS, preferred_element_type=F32)
    ct, st = ct_ref[...], st_ref[...]
    w = DIFF_W
    qdt_ref[0] = (_rope_sublanes(zt[0:w], ct, st) * (DIFF_D ** -0.5 * LOG2E)).astype(BF16)
    qft_ref[0] = (zt[w:2 * w] * (HEAD_DIM ** -0.5 * LOG2E)).astype(BF16)
    kd = _rope_sublanes(zt[2 * w:3 * w], ct, st)
    kf = zt[3 * w:4 * w]
    kdt_ref[0] = kd
    kft_ref[0] = kf
    kdb_ref[...] = kd.T.astype(BF16)
    kfb_ref[...] = kf.T.astype(BF16)
    vd = zt[4 * w:5 * w]
    vf = zt[5 * w:6 * w]
    vdt_ref[0] = vd
    vft_ref[0] = vf
    ones = jnp.ones((BF16_ROWS, tk), BF16)
    for h in range(N_DIFF_HEADS):
        for t in range(tm // tk):
            v1d_ref[0, h, t, 0:HEAD_DIM, :] = vd[HEAD_DIM * h:HEAD_DIM * (h + 1), t * tk:(t + 1) * tk].astype(BF16)
            v1d_ref[0, h, t, HEAD_DIM:V_ROWS, :] = ones
            v1f_ref[0, h, t, 0:HEAD_DIM, :] = vf[HEAD_DIM * h:HEAD_DIM * (h + 1), t * tk:(t + 1) * tk].astype(BF16)
            v1f_ref[0, h, t, HEAD_DIM:V_ROWS, :] = ones
    flt = lax.dot_general(wtfl_ref[...], xb, NT_DIMS, preferred_element_type=F32)
    logft_ref[0] = _log_sigmoid(flt[:N_FOX_HEADS] + btfl_ref[...])


def _project(x2d, n_batch, seq, w, tables, tm, tk):
    n, d_model = x2d.shape
    nt = seq // tm
    ct, st = tables
    full = lambda a: pl.BlockSpec(a.shape, lambda r: (0,) * a.ndim)
    row512 = pl.BlockSpec((tm, DIFF_W), lambda r: (r, 0))
    tabt = pl.BlockSpec((SUBLANES, tm), lambda r: (0, r % nt))
    ft_spec = pl.BlockSpec((1, DIFF_W, tm), lambda r: (r // nt, 0, r % nt))
    v1_spec = pl.BlockSpec((1, N_DIFF_HEADS, tm // tk, V_ROWS, tk), lambda r: (r // nt, 0, r % nt, 0, 0))
    lt_spec = pl.BlockSpec((1, N_FOX_HEADS, tm), lambda r: (r // nt, 0, r % nt))
    sds = jax.ShapeDtypeStruct
    return pl.pallas_call(
        functools.partial(_proj_kernel, tm=tm, tk=tk),
        grid=(n // tm,),
        in_specs=[pl.BlockSpec((tm, d_model), lambda r: (r, 0)),
                  full(w["wt"]), full(w["wtfl"]), full(w["btfl"]), tabt, tabt],
        out_specs=[ft_spec] * 6 + [row512, row512, v1_spec, v1_spec, lt_spec],
        out_shape=[sds((n_batch, DIFF_W, seq), BF16)] * 2 + [sds((n_batch, DIFF_W, seq), F32)] * 4
                  + [sds((n, DIFF_W), BF16)] * 2
                  + [sds((n_batch, N_DIFF_HEADS, seq // tk, V_ROWS, tk), BF16)] * 2
                  + [sds((n_batch, N_FOX_HEADS, seq), F32)],
        compiler_params=_cparams(("parallel",)),
        name="in_proj",
    )(x2d, w["wt"], w["wtfl"], w["btfl"], ct, st)


def _lane_cumsum(x):
    lane = lax.broadcasted_iota(jnp.int32, x.shape, 1)
    s = 1
    while s < LANES:
        x = x + jnp.where(lane >= s, pltpu.roll(x, s, 1), 0.0)
        s *= 2
    return x


def _lane_rev_cumsum(x):
    lane = lax.broadcasted_iota(jnp.int32, x.shape, 1)
    s = 1
    while s < LANES:
        x = x + jnp.where(lane < LANES - s, pltpu.roll(x, LANES - s, 1), 0.0)
        s *= 2
    return x


def _cumsum_kernel(lft_ref, ct_ref, terms_ref, *, seq):
    carry = jnp.zeros((N_FOX_HEADS, 1), F32)
    for t in range(seq // LANES):
        x = _lane_cumsum(lft_ref[0, :, t * LANES:(t + 1) * LANES]) + carry
        ct_ref[0, :, t * LANES:(t + 1) * LANES] = x
        for n, term in enumerate(_split3(-LOG2E * x)):
            terms_ref[0, n, :, t * LANES:(t + 1) * LANES] = term.astype(F32)
        carry = x[:, LANES - 1:LANES]


def _cumsum_t(logft):
    nb, nh, seq = logft.shape
    spec = pl.BlockSpec((1, nh, seq), lambda b: (b, 0, 0))
    return pl.pallas_call(
        functools.partial(_cumsum_kernel, seq=seq),
        grid=(nb,), in_specs=[spec],
        out_specs=[spec, pl.BlockSpec((1, 3, nh, seq), lambda b: (b, 0, 0, 0))],
        out_shape=[jax.ShapeDtypeStruct(logft.shape, F32), jax.ShapeDtypeStruct((nb, 3, nh, seq), F32)],
        compiler_params=_cparams(("parallel",)),
        name="logf_cumsum",
    )(logft)


def _block_diag_q(qt, nblk):
    rows_per = LANES // nblk
    rb = lax.broadcasted_iota(jnp.int32, qt.shape, 0) // rows_per
    zero = jnp.zeros_like(qt)
    return jnp.concatenate([jnp.where(rb == j, qt, zero) for j in range(nblk)], axis=1)


def _scores(j, k_ref, qbd_ref, s_ref, tk, kbias_ref=None):
    start = pl.multiple_of(j * tk, tk)
    k = k_ref[0, pl.ds(start, tk), :]
    if kbias_ref is not None:
        k = jnp.concatenate([k, kbias_ref[0, 0, pl.ds(start, tk), :]], axis=1)
    s_ref[...] = jnp.dot(k, qbd_ref[...], preferred_element_type=F32)


def _softmax_pv(j, s_ref, vt_ref, acc_ref, m_ref, *, tq, nblk, blk_head, mask_shift=None, cq=None):
    tk = s_ref.shape[0]
    masked = mask_shift is not None
    if masked:
        keep = (lax.broadcasted_iota(jnp.int32, (tk, tq), 0)
                <= lax.broadcasted_iota(jnp.int32, (tk, tq), 1) + mask_shift)
    for blk in range(nblk):
        sb = s_ref[:, blk * tq:(blk + 1) * tq]
        if masked:
            sb = jnp.where(keep, sb, NEG_INF)
        m_old = m_ref[blk, 0:1, :]
        smax = jnp.max(sb, axis=0, keepdims=True)
        if cq is not None:
            m_new = jnp.maximum(m_old, smax + cq[blk])
            shift = m_new - cq[blk]
        else:
            m_new = jnp.maximum(m_old, smax)
            shift = m_new
        p = jnp.exp2(sb - shift).astype(BF16)
        alpha = jnp.exp2(m_old - m_new)
        vt = vt_ref[0, blk_head[blk], j]
        acc_ref[blk] = acc_ref[blk] * alpha + jnp.dot(vt, p, preferred_element_type=F32)
        m_ref[blk] = jnp.broadcast_to(m_new, (SUBLANES, tq))


def _causal_sweep(i, scores, update):
    scores(0, 0)

    def body(jj, carry):
        j = N_SCORE_BUFS * jj
        for r in range(N_SCORE_BUFS):
            scores(j + r + 1, (r + 1) % N_SCORE_BUFS)
            update(j + r, r, False)
        return carry

    lax.fori_loop(0, i // N_SCORE_BUFS, body, 0)
    rem = i % N_SCORE_BUFS
    j0 = i - rem
    for r in range(N_SCORE_BUFS - 1):
        @pl.when(r < rem)
        def _():
            scores(j0 + r + 1, r + 1)
            update(j0 + r, r, False)

    for r in range(N_SCORE_BUFS):
        @pl.when(rem == r)
        def _():
            update(i, r, True)


def _attn_init(acc_ref, m_ref):
    acc_ref[...] = jnp.zeros(acc_ref.shape, F32)
    m_ref[...] = jnp.full(m_ref.shape, NEG_INF, F32)


def _attn_kernel(lq1_ref, lk1_ref, lq2_ref, lk2_ref, gain_ref, qdt_ref, kd_ref, v1d_ref, qft_ref, kf_ref, v1f_ref,
                 kbias_ref, cqt_ref, od_ref, of_ref, accd_ref, md_ref, qbdd_ref, accf_ref, mf_ref, qbdf_ref, *s_refs,
                 tq, tk, lam_init):
    i = pl.program_id(2)
    ratio = tk // tq
    n_full = i // ratio
    mask_shift = (i % ratio) * tq
    nd, nf = 4, 2
    sd_refs, sf_refs = s_refs[:N_SCORE_BUFS], s_refs[N_SCORE_BUFS:]
    qbdd_ref[...] = _block_diag_q(qdt_ref[0], nd)
    qbdf_ref[0:LANES, :] = _block_diag_q(qft_ref[0], nf)
    r = lax.broadcasted_iota(jnp.int32, (LANES, nf * tq), 0)
    cb = lax.broadcasted_iota(jnp.int32, (LANES, nf * tq), 1) // tq
    qbdf_ref[LANES:2 * LANES, :] = jnp.where((r < 3 * nf) & (r // 3 == cb), 1.0, 0.0).astype(BF16)
    _attn_init(accd_ref, md_ref)
    _attn_init(accf_ref, mf_ref)
    cq = [cqt_ref[0, 0, hh:hh + 1, :] * LOG2E for hh in range(nf)]

    def scores(j, slot):
        _scores(j, kd_ref, qbdd_ref, sd_refs[slot], tk)
        _scores(j, kf_ref, qbdf_ref, sf_refs[slot], tk, kbias_ref)

    def update(j, slot, masked):
        shift = mask_shift if masked else None
        _softmax_pv(j, sd_refs[slot], v1d_ref, accd_ref, md_ref, tq=tq, nblk=nd, blk_head=(0, 0, 1, 1),
                    mask_shift=shift)
        _softmax_pv(j, sf_refs[slot], v1f_ref, accf_ref, mf_ref, tq=tq, nblk=nf, blk_head=(0, 1), mask_shift=shift,
                    cq=cq)

    _causal_sweep(n_full, scores, update)

    lam = _lambda_full(lq1_ref, lk1_ref, lq2_ref, lk2_ref, lam_init)
    outs = []
    for hh in range(2):
        a1 = accd_ref[2 * hh]
        a2 = accd_ref[2 * hh + 1]
        o = a1[0:HEAD_DIM] / a1[HEAD_DIM:HEAD_DIM + 1] - lam * (a2[0:HEAD_DIM] / a2[HEAD_DIM:HEAD_DIM + 1])
        ms = jnp.mean(o * o, axis=0, keepdims=True)
        outs.append(o * lax.rsqrt(ms + RMS_EPS))
    o = jnp.concatenate(outs, axis=0).T
    od_ref[0] = (o * gain_ref[...] * (1.0 - lam_init)).astype(od_ref.dtype)
    outs = []
    for hh in range(nf):
        a = accf_ref[hh]
        outs.append(a[0:HEAD_DIM] / a[HEAD_DIM:HEAD_DIM + 1])
    of_ref[0] = jnp.concatenate(outs, axis=0).T.astype(of_ref.dtype)


def _prompt_attention(qdt, kdb, v1d, qft, kfb, v1f, kbias_pairs, cqt_pairs, lambdas, gain128, lam_init, tq, tk):
    nb, _, seq = qdt.shape
    n_pairs = DIFF_W // LANES
    assert tk % tq == 0
    qt_spec = pl.BlockSpec((1, LANES, tq), lambda b, p, i: (b, p, i))
    k_spec = pl.BlockSpec((1, seq, LANES), lambda b, p, i: (b, 0, p))
    vt_spec = pl.BlockSpec((1, 2, seq // tk, V_ROWS, tk), lambda b, p, i: (b, p, 0, 0, 0))
    kbias_spec = pl.BlockSpec((1, 1, seq, LANES), lambda b, p, i: (b, p, 0, 0))
    cq_spec = pl.BlockSpec((1, 1, 2, tq), lambda b, p, i: (b, p, 0, i))
    o_spec = pl.BlockSpec((1, tq, LANES), lambda b, p, i: (b, i, p))
    small = lambda a: pl.BlockSpec(a.shape, lambda b, p, i: (0,) * a.ndim)

    def state(nblk, k_width):
        return [pltpu.VMEM((nblk, V_ROWS, tq), F32), pltpu.VMEM((nblk, SUBLANES, tq), F32),
                pltpu.VMEM((k_width, nblk * tq), BF16)]

    score_bufs = lambda nblk: [pltpu.VMEM((tk, nblk * tq), F32)] * N_SCORE_BUFS
    return pl.pallas_call(
        functools.partial(_attn_kernel, tq=tq, tk=tk, lam_init=lam_init),
        grid=(nb, n_pairs, seq // tq),
        in_specs=[small(lambdas[0])] * 4 + [small(gain128), qt_spec, k_spec, vt_spec, qt_spec, k_spec, vt_spec,
                                            kbias_spec, cq_spec],
        out_specs=[o_spec, o_spec],
        out_shape=[jax.ShapeDtypeStruct((nb, seq, DIFF_W), BF16), jax.ShapeDtypeStruct((nb, seq, FOX_W), BF16)],
        scratch_shapes=state(4, LANES) + state(2, 2 * LANES) + score_bufs(4) + score_bufs(2),
        compiler_params=_cparams(("parallel", "parallel", "arbitrary")),
        name="prompt_attention",
    )(*lambdas, gain128, qdt, kdb, v1d, qft, kfb, v1f, kbias_pairs, cqt_pairs)


def _row_softmax_update(s, vts, m_ref, l_ref, acc_ref):
    m_old = m_ref[:, 0:1]
    m_new = jnp.maximum(m_old, jnp.max(s, axis=1, keepdims=True))
    p = jnp.exp2(s - m_new)
    alpha = jnp.exp2(m_old - m_new)
    l_new = alpha * l_ref[:, 0:1] + jnp.sum(p, axis=1, keepdims=True)
    pb = p.astype(BF16)
    pv = sum(lax.dot_general(pb[:, i * LANES:(i + 1) * LANES], vt.astype(BF16), NT_DIMS, preferred_element_type=F32)
             for i, vt in enumerate(vts))
    acc_ref[...] = alpha * acc_ref[...] + pv
    m_ref[...] = jnp.broadcast_to(m_new, m_ref.shape)
    l_ref[...] = jnp.broadcast_to(l_new, l_ref.shape)


def _sample_attn_kernel(pt_ref, lq1_ref, lk1_ref, lq2_ref, lk2_ref, gain_ref, qd_ref, qf_ref, *rest,
                        dec_seq, lam_init, pages_per_step):
    pp = pages_per_step
    dk_refs, dv_refs, fk_refs, fv_refs, lf_refs = (rest[i * pp:(i + 1) * pp] for i in range(5))
    (sdk_ref, sdv_ref, sfk_ref, sfv_ref, slf_ref, od_ref, of_ref,
     md_ref, ld_ref, accd_ref, mf_ref, lfs_ref, accf_ref, carry_ref, cself_ref) = rest[5 * pp:]
    j = pl.program_id(1)
    rd = dec_seq * 2 * N_DIFF_HEADS
    rf = dec_seq * N_FOX_HEADS
    def scores(q_ref, kts):
        q = q_ref[0]
        return jnp.concatenate([jnp.dot(q, kt.astype(BF16), preferred_element_type=F32) for kt in kts], axis=1)

    @pl.when(j == 0)
    def _():
        b = pl.program_id(0)
        n_self = sdk_ref.shape[1]
        lane_d = lax.broadcasted_iota(jnp.int32, (rd, n_self), 1)
        t_d = lax.broadcasted_iota(jnp.int32, (rd, n_self), 0) // (2 * N_DIFF_HEADS)
        lane_f = lax.broadcasted_iota(jnp.int32, (rf, n_self), 1)
        t_f = lax.broadcasted_iota(jnp.int32, (rf, n_self), 0) // N_FOX_HEADS
        md_ref[...] = jnp.full(md_ref.shape, NEG_INF, F32)
        ld_ref[...] = jnp.zeros(ld_ref.shape, F32)
        accd_ref[...] = jnp.zeros(accd_ref.shape, F32)
        mf_ref[...] = jnp.full(mf_ref.shape, NEG_INF, F32)
        lfs_ref[...] = jnp.zeros(lfs_ref.shape, F32)
        accf_ref[...] = jnp.zeros(accf_ref.shape, F32)
        carry_ref[...] = jnp.zeros(carry_ref.shape, F32)
        own_d = (lane_d >= b * dec_seq) & (lane_d <= b * dec_seq + t_d)
        sd = jnp.where(own_d, scores(qd_ref, [sdk_ref[...]]), NEG_INF)
        _row_softmax_update(sd, [sdv_ref[...]], md_ref, ld_ref, accd_ref)
        lf = jnp.concatenate([slf_ref[...]] * dec_seq, axis=0)
        own_f = (lane_f >= b * dec_seq) & (lane_f < (b + 1) * dec_seq)
        incl = _lane_cumsum(jnp.where(own_f, lf, 0.0))
        cs_t = jnp.sum(jnp.where(lane_f == b * dec_seq + t_f, incl, 0.0), axis=1, keepdims=True)
        cself_ref[...] = jnp.broadcast_to(cs_t, cself_ref.shape)
        sf = scores(qf_ref, [sfk_ref[...]]) + (cs_t - incl) * LOG2E
        sf = jnp.where(own_f & (lane_f <= b * dec_seq + t_f), sf, NEG_INF)
        _row_softmax_update(sf, [sfv_ref[...]], mf_ref, lfs_ref, accf_ref)

    _row_softmax_update(scores(qd_ref, [r[0] for r in dk_refs]), [r[0] for r in dv_refs], md_ref, ld_ref, accd_ref)
    cself = cself_ref[:, 0:1]
    carry = carry_ref[:, 0:1]
    biases = []
    for r in lf_refs:
        lf = jnp.concatenate([r[0]] * dec_seq, axis=0)
        suffix = _lane_rev_cumsum(lf)
        biases.append(cself + carry + (suffix - lf))
        carry = carry + suffix[:, 0:1]
    carry_ref[...] = jnp.broadcast_to(carry, carry_ref.shape)
    sf = scores(qf_ref, [r[0] for r in fk_refs]) + jnp.concatenate(biases, axis=1) * LOG2E
    _row_softmax_update(sf, [r[0] for r in fv_refs], mf_ref, lfs_ref, accf_ref)

    @pl.when(j == pl.num_programs(1) - 1)
    def _():
        lam = _lambda_full(lq1_ref, lk1_ref, lq2_ref, lk2_ref, lam_init)
        width = accd_ref.shape[1]
        accn = accd_ref[...] / ld_ref[:, 0:1]
        col_h = lax.broadcasted_iota(jnp.int32, (rd, width), 1) // HEAD_DIM
        row = lax.broadcasted_iota(jnp.int32, (rd, width), 0)
        accn = jnp.where(col_h == (row % (2 * N_DIFF_HEADS)) // 2, accn, 0.0)
        sr = lax.broadcasted_iota(jnp.int32, (BF16_ROWS, rd), 0)
        sc = lax.broadcasted_iota(jnp.int32, (BF16_ROWS, rd), 1)
        o_maps = []
        for mp in range(2):
            sel = ((sc // (2 * N_DIFF_HEADS) == sr) & (sc % 2 == mp)).astype(BF16)
            o_maps.append(_dot_sel_lhs(sel, accn))
        od = o_maps[0] - lam * o_maps[1]
        gi = lax.broadcasted_iota(jnp.int32, (width, width), 0) // HEAD_DIM
        gj = lax.broadcasted_iota(jnp.int32, (width, width), 1) // HEAD_DIM
        gmat = jnp.where(gi == gj, 1.0 / HEAD_DIM, 0.0).astype(BF16)
        ms = _dot_sel_rhs(od * od, gmat)
        od_ref[0] = (od * lax.rsqrt(ms + RMS_EPS) * gain_ref[...] * (1.0 - lam_init)).astype(od_ref.dtype)

        accfn = accf_ref[...] / lfs_ref[:, 0:1]
        col_hf = lax.broadcasted_iota(jnp.int32, (rf, width), 1) // HEAD_DIM
        row_f = lax.broadcasted_iota(jnp.int32, (rf, width), 0)
        accfn = jnp.where(col_hf == row_f % N_FOX_HEADS, accfn, 0.0)
        srf = lax.broadcasted_iota(jnp.int32, (BF16_ROWS, rf), 0)
        scf = lax.broadcasted_iota(jnp.int32, (BF16_ROWS, rf), 1)
        self_ = (scf // N_FOX_HEADS == srf).astype(BF16)
        of_ref[0] = _dot_sel_lhs(self_, accfn).astype(of_ref.dtype)


def _sample_attention(page_table, qd_bd, qf_bd, caches, selfs, lambdas, gain512, lam_init, dec_seq):
    nb, n_pages = page_table.shape
    page = caches[0].shape[2]
    assert page == LANES
    rd, rf = qd_bd.shape[1], qf_bd.shape[1]
    pp = next(c for c in (8, 4, 2, 1) if n_pages % c == 0)
    n_steps = n_pages // pp

    def page_map(slot):
        return lambda b, j, pt: (pt[b, n_pages - 1 - (j * pp + slot)], 0, 0)

    per_b = lambda b, j, pt: (b, 0, 0)
    small = lambda a: pl.BlockSpec(a.shape, lambda b, j, pt: (0,) * a.ndim)
    b_spec = lambda a: pl.BlockSpec((1,) + a.shape[1:], per_b)
    out_spec = pl.BlockSpec((1, BF16_ROWS, DIFF_W), per_b)
    cache_specs, cache_args = [], []
    for a in caches:
        for slot in range(pp):
            cache_specs.append(pl.BlockSpec((1,) + a.shape[1:], page_map(slot)))
            cache_args.append(a)
    grid_spec = pltpu.PrefetchScalarGridSpec(
        num_scalar_prefetch=1,
        grid=(nb, n_steps),
        in_specs=[small(lambdas[0])] * 4 + [small(gain512), b_spec(qd_bd), b_spec(qf_bd)]
                 + cache_specs + [small(a) for a in selfs],
        out_specs=[out_spec, out_spec],
        scratch_shapes=[pltpu.VMEM((rd, LANES), F32), pltpu.VMEM((rd, LANES), F32), pltpu.VMEM((rd, DIFF_W), F32),
                        pltpu.VMEM((rf, LANES), F32), pltpu.VMEM((rf, LANES), F32), pltpu.VMEM((rf, FOX_W), F32),
                        pltpu.VMEM((rf, LANES), F32), pltpu.VMEM((rf, LANES), F32)],
    )
    return pl.pallas_call(
        functools.partial(_sample_attn_kernel, dec_seq=dec_seq, lam_init=lam_init, pages_per_step=pp),
        grid_spec=grid_spec,
        out_shape=[jax.ShapeDtypeStruct((nb, BF16_ROWS, DIFF_W), BF16)] * 2,
        compiler_params=_cparams(("parallel", "arbitrary")),
        name="sample_attention",
    )(page_table, *lambdas, gain512, qd_bd, qf_bd, *cache_args, *selfs)


def _merge_kernel(x_ref, od_ref, of_ref, wo_ref, g_ref, b_ref, wr_ref, rb_ref, h_ref, hb_ref, gt_ref, *, alpha):
    o = jnp.concatenate([od_ref[...], of_ref[...]], axis=1)
    mix = jnp.dot(o, wo_ref[...], preferred_element_type=F32)
    h = _layer_norm(alpha * x_ref[...] + mix, g_ref[...], b_ref[...])
    h_ref[...] = h
    hb_ref[...] = h.astype(BF16)

    tm = h.shape[0]
    h1, h2, _ = _split3(h)
    w1, w2, _ = _split3(wr_ref[...])
    mm = lambda a, b: jnp.dot(a, b, preferred_element_type=F32)
    logits = (mm(h1, w1) + (mm(h1, w2) + mm(h2, w1))).T[:N_EXPERTS]
    scores = jax.nn.sigmoid(logits)
    choice = scores + rb_ref[...]
    member = lax.broadcasted_iota(jnp.int32, (GROUP_SIZE, tm), 0)
    blocks, gscore = [], []
    for g in range(N_EXPERT_GROUPS):
        blk = choice[g * GROUP_SIZE:(g + 1) * GROUP_SIZE]
        m1 = jnp.max(blk, axis=0, keepdims=True)
        first = jnp.min(jnp.where(blk == m1, member, GROUP_SIZE), axis=0, keepdims=True)
        m2 = jnp.max(jnp.where(member == first, -jnp.inf, blk), axis=0, keepdims=True)
        blocks.append(blk)
        gscore.append(m1 + m2)
    masked = []
    for g in range(N_EXPERT_GROUPS):
        rank = jnp.zeros((1, tm), jnp.int32)
        for g2 in range(N_EXPERT_GROUPS):
            if g2 == g:
                continue
            beats = (gscore[g2] > gscore[g]) | ((gscore[g2] == gscore[g]) & (g2 < g))
            rank = rank + beats.astype(jnp.int32)
        masked.append(jnp.where(rank < TOPK_GROUPS, blocks[g], NEG_INF))
    cnts = [jnp.zeros((GROUP_SIZE, tm), jnp.int32) for _ in range(N_EXPERT_GROUPS)]
    for e2 in range(N_EXPERTS):
        g2, k2 = divmod(e2, GROUP_SIZE)
        r = masked[g2][k2:k2 + 1]
        for g in range(N_EXPERT_GROUPS):
            v = masked[g]
            if g2 < g:
                beats = r >= v
            elif g2 > g:
                beats = r > v
            else:
                beats = (r > v) | ((r == v) & (k2 < member))
            cnts[g] = cnts[g] + beats.astype(jnp.int32)
    cnt = jnp.concatenate(cnts, axis=0)
    w = jnp.where(cnt < TOP_K, scores, 0.0)
    gt_ref[...] = w / (jnp.sum(w, axis=0, keepdims=True) + 1e-20) * ROUTED_SCALE


def _merge(x2d, od, of, wo_bf, g, b, wr_pad, rb, alpha, tm):
    n, d = x2d.shape
    row = lambda w: pl.BlockSpec((tm, w), lambda r: (r, 0))
    full = lambda a: pl.BlockSpec(a.shape, lambda r: (0,) * a.ndim)
    return pl.pallas_call(
        functools.partial(_merge_kernel, alpha=alpha),
        grid=(n // tm,),
        in_specs=[row(d), row(DIFF_W), row(FOX_W), full(wo_bf), full(g), full(b), full(wr_pad), full(rb)],
        out_specs=[row(d), row(d), pl.BlockSpec((N_EXPERTS, tm), lambda r: (0, r))],
        out_shape=[jax.ShapeDtypeStruct((n, d), F32), jax.ShapeDtypeStruct((n, d), BF16),
                   jax.ShapeDtypeStruct((N_EXPERTS, n), F32)],
        compiler_params=_cparams(("parallel",)),
        name="merge_ln_router",
    )(x2d, od, of, wo_bf, g, b, wr_pad, rb)


def _swiglu(hb, wg, wu):
    a = jnp.dot(hb, wg, preferred_element_type=F32)
    u = jnp.dot(hb, wu, preferred_element_type=F32)
    return jax.nn.silu(a) * u


def _moe_kernel(hb_ref, h_ref, g_ref, wg_ref, wu_ref, wd_ref, wsg_ref, wsu_ref, wsd_ref, ln_g_ref, ln_b_ref,
                y_ref, acc_ref, *, alpha, sub):
    step = pl.program_id(1)
    tn = hb_ref.shape[0]
    per_step = wg_ref.shape[0]

    @pl.when(step == 0)
    def _():
        act = _swiglu(hb_ref[...], wsg_ref[...].astype(BF16), wsu_ref[...].astype(BF16))
        acc_ref[...] = jnp.dot(act.astype(BF16), wsd_ref[...].astype(BF16), preferred_element_type=F32)

    weights = [(wg_ref[k].astype(BF16), wu_ref[k].astype(BF16), wd_ref[k].astype(BF16)) for k in range(per_step)]
    for r0 in range(0, tn, sub):
        gates = g_ref[r0:r0 + sub, :]
        lane = lax.broadcasted_iota(jnp.int32, gates.shape, 1)
        hb = hb_ref[r0:r0 + sub, :]
        contrib = None
        for k, (wg, wu, wd) in enumerate(weights):
            gcol = jnp.sum(jnp.where(lane == step * per_step + k, gates, 0.0), axis=1, keepdims=True)
            act = (_swiglu(hb, wg, wu) * gcol).astype(BF16)
            part = jnp.dot(act, wd, preferred_element_type=F32)
            contrib = part if contrib is None else contrib + part
        acc_ref[r0:r0 + sub, :] += contrib

    @pl.when(step == pl.num_programs(1) - 1)
    def _():
        y_ref[...] = _layer_norm(alpha * h_ref[...] + acc_ref[...], ln_g_ref[...], ln_b_ref[...])


def _moe(hb, h, gates, wg, wu, wd, wsg, wsu, wsd, ln_g, ln_b, alpha, tn):
    n, d = h.shape
    ne = wg.shape[0]
    row = lambda w: pl.BlockSpec((tn, w), lambda i, e: (i, 0))
    full = lambda a: pl.BlockSpec(a.shape, lambda i, e: (0,) * a.ndim)
    assert ne % MOE_EXPERTS_PER_STEP == 0
    per_expert = lambda a: pl.BlockSpec((MOE_EXPERTS_PER_STEP,) + a.shape[1:], lambda i, e: (e, 0, 0))
    return pl.pallas_call(
        functools.partial(_moe_kernel, alpha=alpha, sub=min(tn, MOE_SUB_ROWS)),
        grid=(n // tn, ne // MOE_EXPERTS_PER_STEP),
        in_specs=[row(d), row(d), row(ne), per_expert(wg), per_expert(wu), per_expert(wd),
                  full(wsg), full(wsu), full(wsd), full(ln_g), full(ln_b)],
        out_specs=row(d),
        out_shape=jax.ShapeDtypeStruct((n, d), F32),
        scratch_shapes=[pltpu.VMEM((tn, d), F32)],
        compiler_params=_cparams(("parallel", "arbitrary")),
        name="moe_ln",
    )(hb, h, gates, wg, wu, wd, wsg, wsu, wsd, ln_g, ln_b)


def _pick_tile(n, pref):
    t = min(n, pref)
    assert n % t == 0
    return t


def _feature_major(a):
    n, tokens = a.shape[:2]
    return jnp.moveaxis(a.reshape(n, tokens, -1), 1, 2)


def _token_major(a, feat_shape):
    n, _, tokens = a.shape
    return jnp.moveaxis(a, 1, 2).reshape((n, tokens) + feat_shape)


def kernel(x_prompt, x_sample, cache_diff_k, cache_diff_v, cache_fox_k, cache_fox_v, cache_fox_logf, page_table,
           w_in, b_forget, lambda_q1, lambda_k1, lambda_q2, lambda_k2, subln_gain, w_o, ln1_g, ln1_b, w_router,
           router_bias, w_exp_gate, w_exp_up, w_exp_down, w_sh_gate, w_sh_up, w_sh_down, ln2_g, ln2_b):
    depth = w_in.shape[0]
    nb, seq, d_model = x_prompt.shape
    dec_b, dec_seq, _ = x_sample.shape
    n_pages = page_table.shape[1]
    page = cache_diff_k.shape[2]
    past_len = n_pages * page
    alpha = (2.0 * depth) ** 0.25
    n_s = dec_b * dec_seq

    tq = _pick_tile(seq, 256)
    tk = tq
    tm_p = tk
    tabs_p = _rope_tables(seq, 0, seq)
    tabs_s = _rope_tables(n_s, past_len, dec_seq)

    xp = x_prompt.reshape(nb * seq, d_model)
    xs = x_sample.reshape(n_s, d_model)
    outs_p, outs_s = [], []
    for l in range(depth):
        lam_init = 0.8 - 0.6 * math.exp(-0.3 * l)
        w = w_in[l]
        cuts = [0, DIFF_W, 2 * DIFF_W, 3 * DIFF_W, 3 * DIFF_W + FOX_W, 3 * DIFF_W + 2 * FOX_W, 3 * DIFF_W + 3 * FOX_W]
        wqd, wkd, wvd, wqf, wkf, wvf = [w[:, cuts[i]:cuts[i + 1]] for i in range(6)]
        wfl = w[:, cuts[6]:]
        pw = {
            "wt": jnp.concatenate([wqd, wqf, wkd, wkf, wvd, wvf], axis=1).T.astype(BF16),
            "wtfl": jnp.pad(wfl.T, ((0, BF16_ROWS - N_FOX_HEADS), (0, 0))).astype(BF16),
            "btfl": b_forget[l][:, None],
        }
        lambdas = [v[l][None, :] for v in (lambda_q1, lambda_k1, lambda_q2, lambda_k2)]
        gain = subln_gain[l]
        gain128 = jnp.tile(gain, LANES // HEAD_DIM)[None, :]
        gain512 = jnp.tile(gain, DIFF_W // HEAD_DIM)[None, :]
        wo_bf = w_o[l].astype(BF16)
        wr_pad = jnp.pad(w_router[l], ((0, 0), (0, LANES - N_EXPERTS)))
        rb = router_bias[l][:, None]
        moe_w = (w_exp_gate[l], w_exp_up[l], w_exp_down[l], w_sh_gate[l], w_sh_up[l], w_sh_down[l])
        g1, b1 = ln1_g[l][None, :], ln1_b[l][None, :]
        g2, b2 = ln2_g[l][None, :], ln2_b[l][None, :]

        def ffn(x2d, od, of, tm, tn):
            h, hb, gt = _merge(x2d, od, of, wo_bf, g1, b1, wr_pad, rb, alpha, tm)
            return _moe(hb, h, gt.T, *moe_w, g2, b2, alpha, tn)

        (qdt, qft, kdt, kft, vdt, vft, kdb, kfb, v1d, v1f, logft) = _project(xp, nb, seq, pw, tabs_p, tm_p, tk)
        ct, terms = _cumsum_t(logft)
        n_pairs = N_FOX_HEADS // 2
        cqt_pairs = ct.reshape(nb, n_pairs, 2, seq)
        kbias_pairs = jnp.transpose(terms.reshape(nb, 3, n_pairs, 2, seq), (0, 2, 4, 3, 1)).reshape(nb, n_pairs, seq, 6)
        kbias_pairs = jnp.pad(kbias_pairs, ((0, 0), (0, 0), (0, 0), (0, LANES - 6))).astype(BF16)
        od, of = _prompt_attention(qdt, kdb.reshape(nb, seq, DIFF_W), v1d, qft, kfb.reshape(nb, seq, FOX_W), v1f,
                                   kbias_pairs, cqt_pairs, lambdas, gain128, lam_init, tq, tk)
        xp_new = ffn(xp, od.reshape(nb * seq, DIFF_W), of.reshape(nb * seq, FOX_W),
                     _pick_tile(nb * seq, 256), _pick_tile(nb * seq, 1024))
        outs_p.append((kdt, vdt, kft, vft, logft))

        (sqdt, sqft, skdt, skft, svdt, svft, _, _, _, _, slogft) = _project(xs, 1, n_s, pw, tabs_s, n_s, n_s)
        qd_rows = sqdt[0].T.reshape(dec_b, dec_seq, 2 * N_DIFF_HEADS, DIFF_D)
        eye_hm = jnp.eye(2 * N_DIFF_HEADS, dtype=BF16)
        qd_bd = (qd_rows[:, :, :, None, :] * eye_hm[None, None, :, :, None]).reshape(
            dec_b, dec_seq * 2 * N_DIFF_HEADS, DIFF_W)
        qf_rows = sqft[0].T.reshape(dec_b, dec_seq, N_FOX_HEADS, HEAD_DIM)
        eye_h = jnp.eye(N_FOX_HEADS, dtype=BF16)
        qf_bd = (qf_rows[:, :, :, None, :] * eye_h[None, None, :, :, None]).reshape(
            dec_b, dec_seq * N_FOX_HEADS, FOX_W)
        caches = [_feature_major(c[l]) for c in (cache_diff_k, cache_diff_v, cache_fox_k, cache_fox_v, cache_fox_logf)]
        selfs = [skdt[0], svdt[0], skft[0], svft[0], slogft[0]]
        od_s, of_s = _sample_attention(page_table, qd_bd, qf_bd, caches, selfs, lambdas, gain512, lam_init, dec_seq)
        od_s = od_s[:, :dec_seq].reshape(n_s, DIFF_W)
        of_s = of_s[:, :dec_seq].reshape(n_s, FOX_W)
        xs_new = ffn(xs, od_s, of_s, n_s, n_s)
        outs_s.append(tuple(a.reshape(a.shape[1], dec_b, dec_seq) for a in (skdt, svdt, skft, svft, slogft)))
        xp, xs = xp_new, xs_new

    def stack_p(idx, feat_shape):
        return jnp.stack([_token_major(o[idx], feat_shape) for o in outs_p])

    def stack_s(idx, feat_shape):
        return jnp.stack([jnp.transpose(o[idx], (1, 2, 0)).reshape((dec_b, dec_seq) + feat_shape) for o in outs_s])

    shapes = ((N_DIFF_HEADS, 2, DIFF_D), (N_DIFF_HEADS, HEAD_DIM), (N_FOX_HEADS, HEAD_DIM), (N_FOX_HEADS, HEAD_DIM),
              (N_FOX_HEADS,))
    return (xp.reshape(nb, seq, d_model), xs.reshape(dec_b, dec_seq, d_model),
            *[stack_p(i, s) for i, s in enumerate(shapes)], *[stack_s(i, s) for i, s in enumerate(shapes)])
```
